```python
import math
import jax
import jax.numpy as jnp
from jax import lax
import numpy as np

D_MODEL = 2048
BATCH = 1
SEQ = 16384
DEPTH = 2

EPS = 1e-6
MIX_WIDTH = D_MODEL
S5_WIDTH = MIX_WIDTH // 2
S5_GROUP = 16
S5_GROUPS = S5_WIDTH // S5_GROUP
S5_STATE = 64
S5_DT_MIN = 1e-3
S5_DT_MAX = 1e-1
GDN_HEAD_DIM = 128
GDN_V_HEADS = (MIX_WIDTH - S5_WIDTH) // GDN_HEAD_DIM
GDN_QK_HEADS = GDN_V_HEADS // 2
GDN_QK_W = GDN_QK_HEADS * GDN_HEAD_DIM
GDN_V_W = GDN_V_HEADS * GDN_HEAD_DIM
GDN_CONV = 4
GDN_CHUNK = 64
GLA_HEADS = 4
GLA_DK = MIX_WIDTH // 2
GLA_DV = MIX_WIDTH
GLA_LOWRANK = 16
GLA_TAU = 16.0
GLA_CHUNK = 64
FFN_HIDDEN = ((8 * D_MODEL // 3 + 255) // 256) * 256
IN0_SIZES = (S5_WIDTH, GDN_QK_W, GDN_QK_W, GDN_V_W, GDN_V_W, GDN_V_HEADS, GDN_V_HEADS)
IN1_SIZES = (GLA_DK, GLA_DK, GLA_DV, GLA_DV, GLA_LOWRANK)

kernel_name = "hybrid_s5_gdn_gla_sandwich_adaln"


def _split(y, sizes):
    points = np.cumsum(np.array(sizes))[:-1].tolist()
    return jnp.split(y, points, axis=-1)


def rmsnorm(x, w):
    xf = x.astype(jnp.float32)
    y = xf * lax.rsqrt(jnp.mean(xf * xf, axis=-1, keepdims=True) + EPS)
    return (y * w.astype(jnp.float32)).astype(x.dtype)


def l2norm(x):
    return x * lax.rsqrt(jnp.sum(x * x, axis=-1, keepdims=True) + EPS)


def ada_modulation(c, w, b):
    mod = jax.nn.silu(c) @ w + b
    return jnp.split(mod[:, None, :], 6, axis=-1)


def swiglu(h, w_gate, w_up, w_down):
    return (jax.nn.silu(h @ w_gate) * (h @ w_up)) @ w_down


def causal_dwconv(x, w):
    k = w.shape[0]
    return lax.conv_general_dilated(
        x, w[:, None, :].astype(x.dtype), window_strides=(1,), padding=((k - 1, 0),),
        dimension_numbers=("NWC", "WIO", "NWC"), feature_group_count=x.shape[-1])


def to_chunks(x, c):
    b, l, h = x.shape[:3]
    x = x.reshape((b, l // c, c, h) + x.shape[3:])
    return jnp.moveaxis(x, 3, 1)


def s5_mixer(u, lam_re, lam_im, log_step, b_re, b_im, c_re, c_im, d, glu_w, glu_b):
    f32 = jnp.float32
    bsz, l, _ = u.shape
    uf = u.astype(f32).reshape(bsz, l, S5_GROUPS, S5_GROUP)
    lr, li = lam_re.astype(f32), lam_im.astype(f32)
    dt = jnp.exp(log_step.astype(f32))[:, None]
    mag = jnp.exp(lr * dt)
    ab_re, ab_im = mag * jnp.cos(li * dt), mag * jnp.sin(li * dt)
    den = lr * lr + li * li
    nr, ni = ab_re - 1.0, ab_im
    f_re = (nr * lr + ni * li) / den
    f_im = (ni * lr - nr * li) / den
    br, bi = b_re.astype(f32), b_im.astype(f32)
    bb_re = f_re[..., None] * br - f_im[..., None] * bi
    bb_im = f_re[..., None] * bi + f_im[..., None] * br
    bu_re = jnp.einsum("blgp,gnp->blgn", uf, bb_re)
    bu_im = jnp.einsum("blgp,gnp->blgn", uf, bb_im)
    a_re = jnp.broadcast_to(ab_re, bu_re.shape)
    a_im = jnp.broadcast_to(ab_im, bu_im.shape)

    def combine(e1, e2):
        a1r, a1i, b1r, b1i = e1
        a2r, a2i, b2r, b2i = e2
        return (a2r * a1r - a2i * a1i, a2r * a1i + a2i * a1r,
                a2r * b1r - a2i * b1i + b2r, a2r * b1i + a2i * b1r + b2i)

    _, _, xr, xi = lax.associative_scan(combine, (a_re, a_im, bu_re, bu_im), axis=1)
    y = (jnp.einsum("blgn,gpn->blgp", xr, c_re.astype(f32))
         - jnp.einsum("blgn,gpn->blgp", xi, c_im.astype(f32))
         + d.astype(f32) * uf)
    y = jax.nn.gelu(y.reshape(bsz, l, S5_WIDTH)).astype(u.dtype)
    return y * jax.nn.sigmoid(y @ glu_w + glu_b)


def gated_delta_chunked(q, k, v, g, beta):
    bsz, l, h, _ = q.shape
    dv = v.shape[-1]
    cs = GDN_CHUNK
    q, k, v = to_chunks(q, cs), to_chunks(k, cs), to_chunks(v, cs)
    g = jnp.cumsum(to_chunks(g, cs), axis=-1)
    beta = to_chunks(beta, cs)
    causal = jnp.tril(jnp.ones((cs, cs), bool))
    strict = jnp.tril(jnp.ones((cs, cs), bool), -1)
    decay = jnp.exp(jnp.where(causal, g[..., :, None] - g[..., None, :], -jnp.inf))
    kb = k * beta[..., None]
    lower = jnp.where(strict, jnp.einsum("bhnid,bhnjd->bhnij", kb, k) * decay, 0.0)
    eye = jnp.eye(cs, dtype=q.dtype)
    t_mat = lax.linalg.triangular_solve(lower + eye, jnp.broadcast_to(eye, lower.shape),
                                        left_side=True, lower=True, unit_diagonal=True)
    u_vals = t_mat @ (v * beta[..., None])
    w_keys = t_mat @ (kb * jnp.exp(g)[..., None])
    attn = jnp.where(causal, jnp.einsum("bhnid,bhnjd->bhnij", q, k) * decay, 0.0)
    q_dec = q * jnp.exp(g)[..., None]
    k_dec = k * jnp.exp(g[..., -1:] - g)[..., None]
    g_last = jnp.exp(g[..., -1])
    xs = tuple(jnp.moveaxis(t, 2, 0) for t in (u_vals, w_keys, attn, q_dec, k_dec, g_last))

    def step(s, inp):
        u_c, w_c, a_c, qd, kd, gl = inp
        v_new = u_c - jnp.einsum("bhck,bhkv->bhcv", w_c, s)
        o_c = jnp.einsum("bhck,bhkv->bhcv", qd, s) + jnp.einsum("bhij,bhjv->bhiv", a_c, v_new)
        s = s * gl[..., None, None] + jnp.einsum("bhck,bhcv->bhkv", kd, v_new)
        return s, o_c

    s0 = jnp.zeros((bsz, h, q.shape[-1], dv), q.dtype)
    _, o = lax.scan(step, s0, xs)
    o = jnp.moveaxis(o, 0, 2)
    return o.transpose(0, 2, 3, 1, 4).reshape(bsz, l, h, dv)


def gdn_mixer(q, k, v, z, a, b, conv_w, a_log, dt_bias, norm_w):
    f32 = jnp.float32
    bsz, l, _ = q.shape
    qkv = jax.nn.silu(causal_dwconv(jnp.concatenate([q, k, v], axis=-1), conv_w)).astype(f32)
    q, k, v = _split(qkv, (GDN_QK_W, GDN_QK_W, GDN_V_W))
    rep = GDN_V_HEADS // GDN_QK_HEADS
    q = jnp.repeat(l2norm(q.reshape(bsz, l, GDN_QK_HEADS, GDN_HEAD_DIM)), rep, axis=2) * GDN_HEAD_DIM ** -0.5
    k = jnp.repeat(l2norm(k.reshape(bsz, l, GDN_QK_HEADS, GDN_HEAD_DIM)), rep, axis=2)
    v = v.reshape(bsz, l, GDN_V_HEADS, GDN_HEAD_DIM)
    beta = jax.nn.sigmoid(b.astype(f32))
    g = -jnp.exp(a_log.astype(f32)) * jax.nn.softplus(a.astype(f32) + dt_bias.astype(f32))
    o = gated_delta_chunked(q, k, v, g, beta)
    o = rmsnorm(o, norm_w) * jax.nn.silu(z.astype(f32).reshape(bsz, l, GDN_V_HEADS, GDN_HEAD_DIM))
    return o.reshape(bsz, l, GDN_V_W).astype(z.dtype)


def gla_chunked(q, k, v, gk):
    bsz, l, h, dk = q.shape
    dv = v.shape[-1]
    cs = GLA_CHUNK
    q, k, v, gk = (to_chunks(t, cs) for t in (q, k, v, gk))
    bcum = jnp.cumsum(gk, axis=3)
    q_t = q * jnp.exp(bcum)
    k_t = k * jnp.exp(-bcum)
    causal = jnp.tril(jnp.ones((cs, cs), bool))
    attn = jnp.where(causal, jnp.einsum("bhnik,bhnjk->bhnij", q_t, k_t), 0.0)
    o_intra = attn @ v
    k_dec = k * jnp.exp(bcum[..., -1:, :] - bcum)
    g_last = jnp.exp(bcum[..., -1, :])
    xs = tuple(jnp.moveaxis(t, 2, 0) for t in (q_t, k_dec, v, g_last))

    def step(s, inp):
        qt, kd, vc, gl = inp
        o_c = jnp.einsum("bhck,bhkv->bhcv", qt, s)
        s = s * gl[..., None] + jnp.einsum("bhck,bhcv->bhkv", kd, vc)
        return s, o_c

    s0 = jnp.zeros((bsz, h, dk, dv), q.dtype)
    _, o_inter = lax.scan(step, s0, xs)
    o = o_intra + jnp.moveaxis(o_inter, 0, 2)
    return o.transpose(0, 2, 3, 1, 4).reshape(bsz, l, h, dv)


def gla_mixer(q, k, v, r, g_low, gate_w2, gate_b, norm_w):
    f32 = jnp.float32
    bsz, l, _ = q.shape
    dk, dv = GLA_DK // GLA_HEADS, GLA_DV // GLA_HEADS
    gk = jax.nn.log_sigmoid((g_low @ gate_w2 + gate_b).astype(f32)) / GLA_TAU
    q = q.astype(f32).reshape(bsz, l, GLA_HEADS, dk) * dk ** -0.5
    k = k.astype(f32).reshape(bsz, l, GLA_HEADS, dk)
    v = v.astype(f32).reshape(bsz, l, GLA_HEADS, dv)
    gk = gk.reshape(bsz, l, GLA_HEADS, dk)
    o = gla_chunked(q, k, v, gk)
    o = rmsnorm(o, norm_w) * jax.nn.silu(r.astype(f32).reshape(bsz, l, GLA_HEADS, dv))
    return o.reshape(bsz, l, GLA_DV).astype(r.dtype)


def setup_inputs(seed: int = 0) -> dict:
    key = jax.random.key(seed)
    ks = iter(jax.random.split(key, 64))
    f32 = jnp.float32
    D = D_MODEL

    def nrm(shape, scale):
        return jax.random.normal(next(ks), shape, f32) * scale

    def unif(shape, lo, hi):
        return jax.random.uniform(next(ks), shape, f32, lo, hi)

    def gain(n):
        return 1.0 + nrm((n,), 0.02)

    inp = {}
    inp["x"] = nrm((BATCH, SEQ, D), 1.0)
    inp["c"] = nrm((BATCH, D), 1.0)
    inp["ada_w0"] = nrm((D, 6 * D), D ** -0.5)
    inp["ada_b0"] = nrm((6 * D,), 0.02)
    inp["mix_pre0"] = gain(D)
    inp["mix_post0"] = gain(D)
    inp["ffn_pre0"] = gain(D)
    inp["ffn_post0"] = gain(D)
    inp["w_in0"] = nrm((D, sum(IN0_SIZES)), D ** -0.5)
    n_idx = jnp.arange(S5_STATE, dtype=f32)[None, :]
    inp["s5_lambda_re"] = -0.5 + nrm((S5_GROUPS, S5_STATE), 0.01)
    inp["s5_lambda_im"] = math.pi * n_idx + nrm((S5_GROUPS, S5_STATE), 0.01)
    inp["s5_log_step"] = unif((S5_GROUPS,), math.log(S5_DT_MIN), math.log(S5_DT_MAX))
    inp["s5_b_re"] = nrm((S5_GROUPS, S5_STATE, S5_GROUP), (2 * S5_GROUP) ** -0.5)
    inp["s5_b_im"] = nrm((S5_GROUPS, S5_STATE, S5_GROUP), (2 * S5_GROUP) ** -0.5)
    inp["s5_c_re"] = nrm((S5_GROUPS, S5_GROUP, S5_STATE), (2 * S5_STATE) ** -0.5)
    inp["s5_c_im"] = nrm((S5_GROUPS, S5_GROUP, S5_STATE), (2 * S5_STATE) ** -0.5)
    inp["s5_d"] = nrm((S5_GROUPS, S5_GROUP), 0.5)
    inp["s5_glu_w"] = nrm((S5_WIDTH, S5_WIDTH), S5_WIDTH ** -0.5)
    inp["s5_glu_b"] = nrm((S5_WIDTH,), 0.02)
    inp["gdn_conv_w"] = nrm((GDN_CONV, 2 * GDN_QK_W + GDN_V_W), GDN_CONV ** -0.5)
    inp["gdn_a_log"] = jnp.log(unif((GDN_V_HEADS,), 1.0, 16.0))
    dt = jnp.exp(unif((GDN_V_HEADS,), math.log(1e-3), math.log(1e-1)))
    inp["gdn_dt_bias"] = dt + jnp.log(-jnp.expm1(-dt))
    inp["gdn_norm_w"] = gain(GDN_HEAD_DIM)
    inp["w_out0"] = nrm((MIX_WIDTH, D), MIX_WIDTH ** -0.5)
    inp["ffn_gate0"] = nrm((D, FFN_HIDDEN), D ** -0.5)
    inp["ffn_up0"] = nrm((D, FFN_HIDDEN), D ** -0.5)
    inp["ffn_down0"] = nrm((FFN_HIDDEN, D), FFN_HIDDEN ** -0.5)
    inp["ada_w1"] = nrm((D, 6 * D), D ** -0.5)
    inp["ada_b1"] = nrm((6 * D,), 0.02)
    inp["mix_pre1"] = gain(D)
    inp["mix_post1"] = gain(D)
    inp["ffn_pre1"] = gain(D)
    inp["ffn_post1"] = gain(D)
    inp["w_in1"] = nrm((D, sum(IN1_SIZES)), D ** -0.5)
    inp["gla_gate_w2"] = nrm((GLA_LOWRANK, GLA_DK), GLA_LOWRANK ** -0.5)
    inp["gla_gate_b"] = nrm((GLA_DK,), 0.1)
    inp["gla_norm_w"] = gain(GLA_DV // GLA_HEADS)
    inp["w_out1"] = nrm((GLA_DV, D), GLA_DV ** -0.5)
    inp["ffn_gate1"] = nrm((D, FFN_HIDDEN), D ** -0.5)
    inp["ffn_up1"] = nrm((D, FFN_HIDDEN), D ** -0.5)
    inp["ffn_down1"] = nrm((FFN_HIDDEN, D), FFN_HIDDEN ** -0.5)
    return inp


def reference(x, c, ada_w0, ada_b0, mix_pre0, mix_post0, ffn_pre0, ffn_post0, w_in0,
              s5_lambda_re, s5_lambda_im, s5_log_step, s5_b_re, s5_b_im, s5_c_re, s5_c_im,
              s5_d, s5_glu_w, s5_glu_b, gdn_conv_w, gdn_a_log, gdn_dt_bias, gdn_norm_w,
              w_out0, ffn_gate0, ffn_up0, ffn_down0,
              ada_w1, ada_b1, mix_pre1, mix_post1, ffn_pre1, ffn_post1, w_in1,
              gla_gate_w2, gla_gate_b, gla_norm_w, w_out1, ffn_gate1, ffn_up1, ffn_down1):

    def mixer_even(h):
        u, q, k, v, z, a, b = _split(h @ w_in0, IN0_SIZES)
        y_a = s5_mixer(u, s5_lambda_re, s5_lambda_im, s5_log_step, s5_b_re, s5_b_im,
                       s5_c_re, s5_c_im, s5_d, s5_glu_w, s5_glu_b)
        y_b = gdn_mixer(q, k, v, z, a, b, gdn_conv_w, gdn_a_log, gdn_dt_bias, gdn_norm_w)
        return jnp.concatenate([y_a, y_b.astype(y_a.dtype)], axis=-1) @ w_out0

    def mixer_odd(h):
        q, k, v, r, g_low = _split(h @ w_in1, IN1_SIZES)
        return gla_mixer(q, k, v, r, g_low, gla_gate_w2, gla_gate_b, gla_norm_w) @ w_out1

    layers = (
        (mixer_even, ada_w0, ada_b0, mix_pre0, mix_post0, ffn_pre0, ffn_post0, ffn_gate0, ffn_up0, ffn_down0),
        (mixer_odd, ada_w1, ada_b1, mix_pre1, mix_post1, ffn_pre1, ffn_post1, ffn_gate1, ffn_up1, ffn_down1),
    )
    for i in range(DEPTH):
        mixer, aw, ab, m_pre, m_post, f_pre, f_post, wg, wu, wd = layers[i]
        sh_m, sc_m, gt_m, sh_f, sc_f, gt_f = ada_modulation(c, aw, ab)
        h = rmsnorm(x, m_pre) * (1.0 + sc_m) + sh_m
        x = x + gt_m * rmsnorm(mixer(h), m_post)
        h = rmsnorm(x, f_pre) * (1.0 + sc_f) + sh_f
        x = x + gt_f * rmsnorm(swiglu(h, wg, wu, wd), f_post)
    return x
```

```python
import functools
import math

import jax
import jax.numpy as jnp
from jax import lax
from jax.experimental import pallas as pl
from jax.experimental.pallas import tpu as pltpu

F32 = jnp.float32
BF16 = jnp.bfloat16
EPS = 1e-6

LANES = 128
MXU_N = 256
VMEM_LIMIT = 56 << 20

S5_GROUP = 16
S5_STATE = 64
S5_T = 16
GDN_HEAD_DIM = 128
GDN_V_HEADS = 8
GDN_CONV = 4
CHUNK = 64
GLA_HEADS = 4
GLA_TAU = 16.0


def _params(sem):
    return pltpu.CompilerParams(dimension_semantics=sem, vmem_limit_bytes=VMEM_LIMIT)


def _dot(a, b):
    return jnp.dot(a, b, preferred_element_type=F32)


def _dot_nt(a, b):
    return lax.dot_general(a, b, (((1,), (1,)), ((), ())), preferred_element_type=F32)


def _dot_tn(a, b):
    return lax.dot_general(a, b, (((0,), (0,)), ((), ())), preferred_element_type=F32)


def _silu(x):
    return x * jax.nn.sigmoid(x)


def _prenorm(x, nw, sc, sh):
    ms = jnp.mean(x * x, axis=-1, keepdims=True)
    return (x * lax.rsqrt(ms + EPS) * nw) * (1.0 + sc) + sh


def _postnorm_residual(x, m, pw, gt):
    ms = jnp.mean(m * m, axis=-1, keepdims=True)
    return x + gt * (m * lax.rsqrt(ms + EPS) * pw)


def _shift_rows(x, s, row):
    return jnp.where(row >= s, pltpu.roll(x, s, 0), 0.0)


def _ada_kernel(c_ref, w_ref, b_ref, o_ref):
    s = _silu(c_ref[...])
    o_ref[...] = jnp.sum(s * w_ref[...], axis=0, keepdims=True) + b_ref[...]


def _ada_modulation(c, w, b):
    d, n = w.shape
    tn = 1024
    return pl.pallas_call(
        _ada_kernel,
        grid=(n // tn,),
        in_specs=[pl.BlockSpec((d, 1), lambda j: (0, 0)),
                  pl.BlockSpec((d, tn), lambda j: (0, j)),
                  pl.BlockSpec((1, tn), lambda j: (0, j))],
        out_specs=pl.BlockSpec((1, tn), lambda j: (0, j)),
        out_shape=jax.ShapeDtypeStruct((1, n), F32),
        compiler_params=_params(("arbitrary",)),
        name="ada_modulation",
    )(c.reshape(d, 1), w, b.reshape(1, n))


def _proj_kernel(x_ref, nw_ref, sc_ref, sh_ref, w_ref, ws_ref, o_ref, os_ref, h_ref):
    @pl.when(pl.program_id(1) == 0)
    def _():
        hb = _prenorm(x_ref[...], nw_ref[...], sc_ref[...], sh_ref[...]).astype(BF16)
        h_ref[...] = hb
        os_ref[...] = _dot(hb, ws_ref[...])

    o_ref[...] = _dot(h_ref[...], w_ref[...]).astype(o_ref.dtype)


def _prenorm_project(x, mod, mod_base, norm_w, w_main, w_small, tm, tn):
    l, d = x.shape
    n = w_main.shape[1]
    ns = w_small.shape[1]
    return pl.pallas_call(
        _proj_kernel,
        grid=(l // tm, n // tn),
        in_specs=[pl.BlockSpec((tm, d), lambda i, j: (i, 0)),
                  pl.BlockSpec((1, d), lambda i, j: (0, 0)),
                  pl.BlockSpec((1, d), lambda i, j: (0, mod_base + 1)),
                  pl.BlockSpec((1, d), lambda i, j: (0, mod_base)),
                  pl.BlockSpec((d, tn), lambda i, j: (0, j)),
                  pl.BlockSpec((d, ns), lambda i, j: (0, 0))],
        out_specs=[pl.BlockSpec((tm, tn), lambda i, j: (i, j)),
                   pl.BlockSpec((tm, ns), lambda i, j: (i, 0))],
        out_shape=[jax.ShapeDtypeStruct((l, n), BF16), jax.ShapeDtypeStruct((l, ns), F32)],
        scratch_shapes=[pltpu.VMEM((tm, d), BF16)],
        compiler_params=_params(("arbitrary", "arbitrary")),
        name="prenorm_project",
    )(x, norm_w.reshape(1, d), mod, mod, w_main, w_small)


def _cmul(a, b):
    ar, ai = a
    br, bi = b
    return ar * br - ai * bi, ar * bi + ai * br


def _cumpow(ar, ai, n):
    rr = jnp.broadcast_to(ar, (n,) + ar.shape)
    ii = jnp.broadcast_to(ai, (n,) + ai.shape)
    return lax.associative_scan(_cmul, (rr, ii), axis=0)


def _s5_tables(lam_re, lam_im, log_step, b_re, b_im, c_re, c_im, d_skip, cb, n_hs):
    t = S5_T
    g, n = lam_re.shape
    p = S5_GROUP
    gl = LANES // p
    nl = g // gl
    hp = lax.Precision.HIGHEST
    lr, li = lam_re.astype(F32), lam_im.astype(F32)
    dt = jnp.exp(log_step.astype(F32))[:, None]
    mag = jnp.exp(lr * dt)
    ab_re, ab_im = mag * jnp.cos(li * dt), mag * jnp.sin(li * dt)
    den = lr * lr + li * li
    nr, ni = ab_re - 1.0, ab_im
    f_re = (nr * lr + ni * li) / den
    f_im = (ni * lr - nr * li) / den
    br, bi = b_re.astype(F32), b_im.astype(F32)
    bb_re = f_re[..., None] * br - f_im[..., None] * bi
    bb_im = f_re[..., None] * bi + f_im[..., None] * br
    p_re, p_im = _cumpow(ab_re, ab_im, t)
    pw_re = jnp.concatenate([jnp.ones((1, g, n), F32), p_re], axis=0)
    pw_im = jnp.concatenate([jnp.zeros((1, g, n), F32), p_im], axis=0)
    cr, ci = c_re.astype(F32), c_im.astype(F32)
    ca_re = cr[None] * pw_re[:, :, None, :] - ci[None] * pw_im[:, :, None, :]
    ca_im = cr[None] * pw_im[:, :, None, :] + ci[None] * pw_re[:, :, None, :]
    k_lag = (jnp.einsum("dgon,gni->dgoi", ca_re[:t], bb_re, precision=hp)
             - jnp.einsum("dgon,gni->dgoi", ca_im[:t], bb_im, precision=hp))
    k_lag = k_lag.at[0].add(d_skip.astype(F32)[:, :, None] * jnp.eye(p, dtype=F32))
    eye_g = jnp.eye(gl, dtype=F32)

    tt = jnp.arange(t)
    lag = tt[None, :] - tt[:, None]
    kt = k_lag[jnp.clip(lag, 0)] * (lag >= 0).astype(F32)[:, :, None, None, None]
    kt = kt.reshape(t, t, nl, gl, p, p).transpose(2, 0, 3, 5, 1, 4)
    w_intra = kt[:, :, :, :, :, None, :] * eye_g[None, None, :, None, None, :, None]
    w_intra = w_intra.reshape(nl, t * LANES, t * LANES).astype(BF16)

    rv_re = jnp.stack([pw_re[t - 1 - i] for i in range(t)])
    rv_im = jnp.stack([pw_im[t - 1 - i] for i in range(t)])
    bd_re = rv_re[..., None] * bb_re[None] - rv_im[..., None] * bb_im[None]
    bd_im = rv_re[..., None] * bb_im[None] + rv_im[..., None] * bb_re[None]

    def _wb(x):
        x = x.reshape(t, nl, gl, n, p).transpose(1, 0, 2, 4, 3)
        return x[:, :, :, :, None, :] * eye_g[None, None, :, None, :, None]

    w_b = jnp.stack([_wb(bd_re), _wb(bd_im)], axis=4)
    w_b = w_b.reshape(nl, t * LANES, 2 * gl * n).astype(BF16)

    def _wc(x):
        x = x.reshape(t, nl, gl, p, n).transpose(1, 2, 4, 0, 3)
        return x[:, :, :, :, None, :] * eye_g[None, :, None, None, :, None]

    w_c = jnp.stack([_wc(ca_re[1:t + 1]), _wc(-ca_im[1:t + 1])], axis=1)
    w_c = w_c.reshape(nl, 2 * gl * n, t * LANES).astype(BF16)

    def _lanes(x):
        x = x.reshape(x.shape[:-2] + (nl, gl * n))
        return jnp.moveaxis(x, -2, 0)

    at = (pw_re[t], pw_im[t])
    sq = [at]
    for _ in range(n_hs - 1):
        sq.append(_cmul(sq[-1], sq[-1]))
    hs = jnp.concatenate([_lanes(jnp.stack([q[0] for q in sq])),
                          _lanes(jnp.stack([q[1] for q in sq]))], axis=-1)
    hs = hs[:, :, None, :]
    c_re_pw, c_im_pw = _cumpow(at[0], at[1], cb)
    pt = jnp.concatenate([_lanes(c_re_pw), _lanes(c_im_pw)], axis=-1)
    return w_intra, w_b, w_c, hs, pt


def _s5_kernel(u_ref, wi_ref, wb_ref, wc_ref, hs_ref, pt_ref, o_ref, uf_ref, u2_ref, yf_ref, carry_ref,
               *, cb, n_hs):
    t = S5_T
    half = (LANES // S5_GROUP) * S5_STATE

    @pl.when(pl.program_id(1) == 0)
    def _():
        carry_ref[...] = jnp.zeros_like(carry_ref)

    uf_ref[...] = u_ref[...].astype(F32)
    for k in range(t):
        u2_ref[:, k * LANES:(k + 1) * LANES] = uf_ref[pl.ds(k, cb, stride=t), :].astype(BF16)
    u2 = u2_ref[...]

    s = _dot(u2, wb_ref[0])
    sr, si = s[:, :half], s[:, half:]
    row = lax.broadcasted_iota(jnp.int32, (cb, half), 0)
    for k in range(n_hs):
        pr, pi = hs_ref[0, k, :, :half], hs_ref[0, k, :, half:]
        shr, shi = _shift_rows(sr, 1 << k, row), _shift_rows(si, 1 << k, row)
        sr, si = sr + pr * shr - pi * shi, si + pr * shi + pi * shr
    cr, ci = carry_ref[:, :half], carry_ref[:, half:]
    ptr, pti = pt_ref[0, :, :half], pt_ref[0, :, half:]
    er = sr + ptr * cr - pti * ci
    ei = si + ptr * ci + pti * cr
    epr = jnp.where(row >= 1, pltpu.roll(er, 1, 0), cr)
    epi = jnp.where(row >= 1, pltpu.roll(ei, 1, 0), ci)
    carry_ref[:, :half] = er[cb - 1:cb]
    carry_ref[:, half:] = ei[cb - 1:cb]
    ep = jnp.concatenate([epr, epi], axis=1).astype(BF16)
    y_inter = _dot(ep, wc_ref[0])

    for tb in range(t * LANES // MXU_N):
        kk = (tb + 1) * MXU_N
        yb = y_inter[:, tb * MXU_N:kk] + _dot(u2[:, :kk], wi_ref[0, :kk, tb * MXU_N:kk])
        yb = jax.nn.gelu(yb, approximate=True)
        for q in range(MXU_N // LANES):
            yf_ref[pl.ds(tb * (MXU_N // LANES) + q, cb, stride=t), :] = yb[:, q * LANES:(q + 1) * LANES]
    o_ref[...] = yf_ref[...].astype(o_ref.dtype)


def _s5_mixer(proj, tables, l, cb, n_hs):
    w_intra, w_b, w_c, hs, pt = tables
    t = S5_T
    nl = w_intra.shape[0]
    rb = cb * t
    kern = functools.partial(_s5_kernel, cb=cb, n_hs=n_hs)
    return pl.pallas_call(
        kern,
        grid=(nl, l // rb),
        in_specs=[pl.BlockSpec((rb, LANES), lambda a, b: (b, a)),
                  pl.BlockSpec((1,) + w_intra.shape[1:], lambda a, b: (a, 0, 0)),
                  pl.BlockSpec((1,) + w_b.shape[1:], lambda a, b: (a, 0, 0)),
                  pl.BlockSpec((1,) + w_c.shape[1:], lambda a, b: (a, 0, 0)),
                  pl.BlockSpec((1,) + hs.shape[1:], lambda a, b: (a, 0, 0, 0)),
                  pl.BlockSpec((1,) + pt.shape[1:], lambda a, b: (a, 0, 0))],
        out_specs=pl.BlockSpec((rb, LANES), lambda a, b: (b, a)),
        out_shape=jax.ShapeDtypeStruct((l, nl * LANES), BF16),
        scratch_shapes=[pltpu.VMEM((rb, LANES), F32),
                        pltpu.VMEM((cb, t * LANES), BF16),
                        pltpu.VMEM((rb, LANES), F32),
                        pltpu.VMEM((1, w_b.shape[2]), F32)],
        compiler_params=_params(("arbitrary", "arbitrary")),
        name="s5_mixer",
    )(proj, w_intra, w_b, w_c, hs, pt)


def _split_bf16(x):
    hi = x.astype(BF16)
    lo = (x - hi.astype(F32)).astype(BF16)
    return hi, lo


def _dot_split(a, b):
    ah, al = _split_bf16(a)
    bh, bl = _split_bf16(b)
    return _dot(ah, bh) + (_dot(ah, bl) + _dot(al, bh))


def _unit_lower_inverse(lows):
    c = lows[0].shape[0]
    nb = len(lows)
    lw = jnp.concatenate(lows, axis=1)
    r = lax.broadcasted_iota(jnp.int32, (c, nb * c), 0)
    q = lax.broadcasted_iota(jnp.int32, (c, nb * c), 1)
    eye_w = jnp.where(q % c == r, 1.0, 0.0)
    rr = lax.broadcasted_iota(jnp.int32, (nb * c, nb * c), 0)
    qq = lax.broadcasted_iota(jnp.int32, (nb * c, nb * c), 1)
    on_diag = (rr // c) == (qq // c)

    def block_diag(xw):
        return jnp.where(on_diag, jnp.concatenate([xw] * nb, axis=0), 0.0)

    acc = eye_w - lw
    x = _dot_split(lw, block_diag(lw))
    steps = int(math.log2(c)) - 1
    for k in range(steps):
        both = _dot_split(jnp.concatenate([x, acc], axis=0), block_diag(x))
        acc = acc + both[c:]
        x = both[:c]
    return [acc[:, i * c:(i + 1) * c] for i in range(nb)]


def _gdn_kernel(q_ref, k_ref, v_ref, z_ref, ab_ref, cw_ref, al_ref, dtb_ref, nw_ref, o_ref,
                xe_ref, qn_ref, kn_ref, vc_ref, gc_ref, beta_ref, s_ref, *, rb):
    hd = GDN_HEAD_DIM
    nh = GDN_V_HEADS
    nqk = nh // 2
    qw = nqk * hd
    c = CHUNK
    halo = 8

    @pl.when(pl.program_id(0) == 0)
    def _():
        s_ref[...] = jnp.zeros_like(s_ref)
        xe_ref[0:halo, :] = jnp.zeros((halo, xe_ref.shape[1]), F32)

    xe_ref[halo:, 0:qw] = q_ref[...].astype(F32)
    xe_ref[halo:, qw:2 * qw] = k_ref[...].astype(F32)
    xe_ref[halo:, 2 * qw:] = v_ref[...].astype(F32)
    acc = cw_ref[GDN_CONV - 1:GDN_CONV, :] * xe_ref[halo:, :]
    for s in range(1, GDN_CONV):
        acc = acc + cw_ref[GDN_CONV - 1 - s:GDN_CONV - s, :] * xe_ref[halo - s:halo - s + rb, :]
    xe_ref[0:halo, :] = xe_ref[rb:rb + halo, :]
    qkv = _silu(acc)
    for p in range(nqk):
        qp = qkv[:, p * hd:(p + 1) * hd]
        qn_ref[:, p * hd:(p + 1) * hd] = qp * lax.rsqrt(jnp.sum(qp * qp, -1, keepdims=True) + EPS) * (hd ** -0.5)
        kp = qkv[:, qw + p * hd:qw + (p + 1) * hd]
        kn_ref[:, p * hd:(p + 1) * hd] = kp * lax.rsqrt(jnp.sum(kp * kp, -1, keepdims=True) + EPS)
    vc_ref[...] = qkv[:, 2 * qw:]

    ab = ab_ref[...]
    g = -jnp.exp(al_ref[...]) * jax.nn.softplus(ab + dtb_ref[...])
    beta_ref[...] = jax.nn.sigmoid(ab)
    row = lax.broadcasted_iota(jnp.int32, (rb, LANES), 0) % c
    for k in range(int(math.log2(c))):
        g = g + jnp.where(row >= (1 << k), pltpu.roll(g, 1 << k, 0), 0.0)
    gc_ref[...] = g

    ri = lax.broadcasted_iota(jnp.int32, (c, c), 0)
    ci = lax.broadcasted_iota(jnp.int32, (c, c), 1)
    causal = ri >= ci
    strict = ri > ci
    nw = nw_ref[...]

    def chunk_body(n, carry):
        r0 = pl.multiple_of(n * c, c)
        gcc = gc_ref[pl.ds(r0, c), :]
        gct = gcc.T
        bet = beta_ref[pl.ds(r0, c), :]
        per_head = []
        for p in range(nqk):
            qp = qn_ref[pl.ds(r0, c), p * hd:(p + 1) * hd]
            kp = kn_ref[pl.ds(r0, c), p * hd:(p + 1) * hd]
            kb16 = kp.astype(BF16)
            qk_kk = _dot_nt(jnp.concatenate([qp, kp], axis=0).astype(BF16), kb16)
            for h in (2 * p, 2 * p + 1):
                gcol = gcc[:, h:h + 1]
                grow = gct[h:h + 1, :]
                dec = jnp.exp(jnp.where(causal, gcol - grow, -jnp.inf))
                bcol = bet[:, nh + h:nh + h + 1]
                low = jnp.where(strict, bcol * qk_kk[c:] * dec, 0.0)
                attn = qk_kk[:c] * dec
                per_head.append((h, qp, kp, gcol, bcol, low, attn))
        t_mats = []
        for i in range(0, nh, 4):
            t_mats += _unit_lower_inverse([ph[5] for ph in per_head[i:i + 4]])
        for (h, qp, kp, gcol, bcol, low, attn), t_mat in zip(per_head, t_mats):
            eg = jnp.exp(gcol)
            vh = vc_ref[pl.ds(r0, c), h * hd:(h + 1) * hd]
            rhs = jnp.concatenate([vh * bcol, kp * (bcol * eg)], axis=1).astype(BF16)
            uw = _dot(t_mat.astype(BF16), rhs)
            st = s_ref[h]
            ws_qs = _dot(jnp.concatenate([uw[:, hd:], qp * eg], axis=0).astype(BF16), st.astype(BF16))
            v_new = uw[:, :hd] - ws_qs[:c]
            v16 = v_new.astype(BF16)
            o = ws_qs[c:] + _dot(attn.astype(BF16), v16)
            g_last = gcc[c - 1:c, h:h + 1]
            k_dec = kp * jnp.exp(g_last - gcol)
            s_ref[h] = st * jnp.exp(g_last) + _dot_tn(k_dec.astype(BF16), v16)
            zh = z_ref[pl.ds(r0, c), h * hd:(h + 1) * hd].astype(F32)
            on = o * lax.rsqrt(jnp.mean(o * o, -1, keepdims=True) + EPS) * nw
            o_ref[pl.ds(r0, c), h * hd:(h + 1) * hd] = (on * _silu(zh)).astype(o_ref.dtype)
        return carry

    lax.fori_loop(0, rb // c, chunk_body, 0)


def _gdn_mixer(proj, ab, conv_w, a_log, dt_bias, norm_w, l, rb):
    hd, nh = GDN_HEAD_DIM, GDN_V_HEADS
    qw = (nh // 2) * hd
    vw = nh * hd
    pad = LANES - nh
    al = jnp.pad(a_log.astype(F32), (0, pad)).reshape(1, LANES)
    dtb = jnp.pad(dt_bias.astype(F32), (0, pad)).reshape(1, LANES)
    kern = functools.partial(_gdn_kernel, rb=rb)
    return pl.pallas_call(
        kern,
        grid=(l // rb,),
        in_specs=[pl.BlockSpec((rb, qw), lambda i: (i, 2)),
                  pl.BlockSpec((rb, qw), lambda i: (i, 3)),
                  pl.BlockSpec((rb, vw), lambda i: (i, 2)),
                  pl.BlockSpec((rb, vw), lambda i: (i, 3)),
                  pl.BlockSpec((rb, LANES), lambda i: (i, 0)),
                  pl.BlockSpec((GDN_CONV, 2 * qw + vw), lambda i: (0, 0)),
                  pl.BlockSpec((1, LANES), lambda i: (0, 0)),
                  pl.BlockSpec((1, LANES), lambda i: (0, 0)),
                  pl.BlockSpec((1, hd), lambda i: (0, 0))],
        out_specs=pl.BlockSpec((rb, vw), lambda i: (i, 0)),
        out_shape=jax.ShapeDtypeStruct((l, vw), BF16),
        scratch_shapes=[pltpu.VMEM((rb + 8, 2 * qw + vw), F32),
                        pltpu.VMEM((rb, qw), F32),
                        pltpu.VMEM((rb, qw), F32),
                        pltpu.VMEM((rb, vw), F32),
                        pltpu.VMEM((rb, LANES), F32),
                        pltpu.VMEM((rb, LANES), F32),
                        pltpu.VMEM((nh, hd, hd), F32)],
        compiler_params=_params(("arbitrary",)),
        name="gdn_mixer",
    )(proj, proj, proj, proj, ab, conv_w.astype(F32), al, dtb, norm_w.astype(F32).reshape(1, hd))


def _gla_kernel(q_ref, k_ref, v_ref, r_ref, gl_ref, w2_ref, gb_ref, nw_ref, o_ref,
                qt_ref, kt_ref, kd_ref, gle_ref, st_ref, *, rb):
    nh = GLA_HEADS
    dk = q_ref.shape[1] // nh
    dv = v_ref.shape[1] // nh
    c = CHUNK

    @pl.when(pl.program_id(0) == 0)
    def _():
        st_ref[...] = jnp.zeros_like(st_ref)

    x = _dot_split(gl_ref[...], w2_ref[...]) + gb_ref[...]
    b = jax.nn.log_sigmoid(x) / GLA_TAU
    row = lax.broadcasted_iota(jnp.int32, b.shape, 0) % c
    for k in range(int(math.log2(c))):
        b = b + jnp.where(row >= (1 << k), pltpu.roll(b, 1 << k, 0), 0.0)
    q = q_ref[...].astype(F32) * (dk ** -0.5)
    kf = k_ref[...].astype(F32)
    qt_ref[...] = (q * jnp.exp(b)).astype(BF16)
    kt_ref[...] = (kf * jnp.exp(-b)).astype(BF16)
    for n in range(rb // c):
        b_last = b[(n + 1) * c - 1:(n + 1) * c, :]
        kd_ref[n * c:(n + 1) * c, :] = (kf[n * c:(n + 1) * c] * jnp.exp(b_last - b[n * c:(n + 1) * c])).astype(BF16)
        gle_ref[n] = jnp.broadcast_to(jnp.exp(b_last), (8, b.shape[1]))

    ri = lax.broadcasted_iota(jnp.int32, (c, c), 0)
    ci = lax.broadcasted_iota(jnp.int32, (c, c), 1)
    causal = ri >= ci
    nw = nw_ref[...]

    def chunk_body(n, carry):
        r0 = pl.multiple_of(n * c, c)
        gle = gle_ref[n]
        for h in range(nh):
            qt = qt_ref[pl.ds(r0, c), h * dk:(h + 1) * dk]
            kt = kt_ref[pl.ds(r0, c), h * dk:(h + 1) * dk]
            kd = kd_ref[pl.ds(r0, c), h * dk:(h + 1) * dk]
            vh = v_ref[pl.ds(r0, c), h * dv:(h + 1) * dv]
            attn = jnp.where(causal, _dot_nt(qt, kt), 0.0)
            st = st_ref[h]
            o = _dot(attn.astype(BF16), vh) + _dot_nt(qt, st.astype(BF16))
            st_ref[h] = st * gle[0:1, h * dk:(h + 1) * dk] + _dot_tn(vh, kd)
            rh = r_ref[pl.ds(r0, c), h * dv:(h + 1) * dv].astype(F32)
            on = o * lax.rsqrt(jnp.mean(o * o, -1, keepdims=True) + EPS) * nw
            o_ref[pl.ds(r0, c), h * dv:(h + 1) * dv] = (on * _silu(rh)).astype(o_ref.dtype)
        return carry

    lax.fori_loop(0, rb // c, chunk_body, 0)


def _gla_mixer(proj, g_low, gate_w2, gate_b, norm_w, l, rb, dk_all, dv_all):
    nh = GLA_HEADS
    lowrank = gate_w2.shape[0]
    w2 = jnp.pad(gate_w2.astype(F32), ((0, LANES - lowrank), (0, 0)))
    kern = functools.partial(_gla_kernel, rb=rb)
    return pl.pallas_call(
        kern,
        grid=(l // rb,),
        in_specs=[pl.BlockSpec((rb, dk_all), lambda i: (i, 0)),
                  pl.BlockSpec((rb, dk_all), lambda i: (i, 1)),
                  pl.BlockSpec((rb, dv_all), lambda i: (i, 1)),
                  pl.BlockSpec((rb, dv_all), lambda i: (i, 2)),
                  pl.BlockSpec((rb, LANES), lambda i: (i, 0)),
                  pl.BlockSpec((LANES, dk_all), lambda i: (0, 0)),
                  pl.BlockSpec((1, dk_all), lambda i: (0, 0)),
                  pl.BlockSpec((1, dv_all // nh), lambda i: (0, 0))],
        out_specs=pl.BlockSpec((rb, dv_all), lambda i: (i, 0)),
        out_shape=jax.ShapeDtypeStruct((l, dv_all), BF16),
        scratch_shapes=[pltpu.VMEM((rb, dk_all), BF16),
                        pltpu.VMEM((rb, dk_all), BF16),
                        pltpu.VMEM((rb, dk_all), BF16),
                        pltpu.VMEM((rb // CHUNK, 8, dk_all), F32),
                        pltpu.VMEM((nh, dv_all // nh, dk_all // nh), F32)],
        compiler_params=_params(("arbitrary",)),
        name="gla_mixer",
    )(proj, proj, proj, proj, g_low, w2, gate_b.astype(F32).reshape(1, dk_all),
      norm_w.astype(F32).reshape(1, dv_all // nh))


def _out0_kernel(x_ref, ya_ref, yb_ref, gw_ref, gb_ref, wa_ref, wb_ref, pw_ref, gt_ref, o_ref):
    ya = ya_ref[...]
    gate = jax.nn.sigmoid(_dot(ya, gw_ref[...]) + gb_ref[...])
    ya = (ya.astype(F32) * gate).astype(BF16)
    m = _dot(ya, wa_ref[...]) + _dot(yb_ref[...], wb_ref[...])
    o_ref[...] = _postnorm_residual(x_ref[...], m, pw_ref[...], gt_ref[...])


def _out1_kernel(x_ref, y_ref, w_ref, pw_ref, gt_ref, o_ref):
    m = _dot(y_ref[...], w_ref[...])
    o_ref[...] = _postnorm_residual(x_ref[...], m, pw_ref[...], gt_ref[...])


def _const_spec(shape):
    return pl.BlockSpec(shape, lambda i: (0,) * len(shape), pipeline_mode=pl.Buffered(1))


def _out_project0(x, ya, yb, glu_w, glu_b, w_out, post_w, mod, tm):
    l, d = x.shape
    wa = ya.shape[1]
    wb = yb.shape[1]
    return pl.pallas_call(
        _out0_kernel,
        grid=(l // tm,),
        in_specs=[pl.BlockSpec((tm, d), lambda i: (i, 0)),
                  pl.BlockSpec((tm, wa), lambda i: (i, 0)),
                  pl.BlockSpec((tm, wb), lambda i: (i, 0)),
                  _const_spec((wa, wa)),
                  _const_spec((1, wa)),
                  pl.BlockSpec((wa, d), lambda i: (0, 0), pipeline_mode=pl.Buffered(1)),
                  pl.BlockSpec((wb, d), lambda i: (1, 0), pipeline_mode=pl.Buffered(1)),
                  _const_spec((1, d)),
                  pl.BlockSpec((1, d), lambda i: (0, 2))],
        out_specs=pl.BlockSpec((tm, d), lambda i: (i, 0)),
        out_shape=jax.ShapeDtypeStruct((l, d), F32),
        compiler_params=_params(("arbitrary",)),
        name="out_project0",
    )(x, ya, yb, glu_w, glu_b.astype(F32).reshape(1, wa), w_out, w_out, post_w.reshape(1, d), mod)


def _out_project1(x, y, w_out, post_w, mod, tm):
    l, d = x.shape
    w = y.shape[1]
    return pl.pallas_call(
        _out1_kernel,
        grid=(l // tm,),
        in_specs=[pl.BlockSpec((tm, d), lambda i: (i, 0)),
                  pl.BlockSpec((tm, w), lambda i: (i, 0)),
                  _const_spec((w, d)),
                  _const_spec((1, d)),
                  pl.BlockSpec((1, d), lambda i: (0, 2))],
        out_specs=pl.BlockSpec((tm, d), lambda i: (i, 0)),
        out_shape=jax.ShapeDtypeStruct((l, d), F32),
        compiler_params=_params(("arbitrary",)),
        name="out_project1",
    )(x, y, w_out, post_w.reshape(1, d), mod)


def _ffn_kernel(x_ref, nw_ref, sc_ref, sh_ref, wg_ref, wu_ref, wd_ref, pw_ref, gt_ref, o_ref, h_ref, acc_ref):
    j = pl.program_id(1)

    @pl.when(j == 0)
    def _():
        h_ref[...] = _prenorm(x_ref[...], nw_ref[...], sc_ref[...], sh_ref[...]).astype(BF16)

    h = h_ref[...]
    a = (_silu(_dot(h, wg_ref[...])) * _dot(h, wu_ref[...])).astype(BF16)
    part = _dot(a, wd_ref[...])

    @pl.when(j == 0)
    def _():
        acc_ref[...] = part

    @pl.when(j > 0)
    def _():
        acc_ref[...] += part

    @pl.when(j == pl.num_programs(1) - 1)
    def _():
        o_ref[...] = _postnorm_residual(x_ref[...], acc_ref[...], pw_ref[...], gt_ref[...])


def _ffn_block(x, mod, pre_w, post_w, w_gate, w_up, w_down, tm, th):
    l, d = x.shape
    hid = w_gate.shape[1]
    return pl.pallas_call(
        _ffn_kernel,
        grid=(l // tm, hid // th),
        in_specs=[pl.BlockSpec((tm, d), lambda i, j: (i, 0)),
                  pl.BlockSpec((1, d), lambda i, j: (0, 0)),
                  pl.BlockSpec((1, d), lambda i, j: (0, 4)),
                  pl.BlockSpec((1, d), lambda i, j: (0, 3)),
                  pl.BlockSpec((d, th), lambda i, j: (0, j)),
                  pl.BlockSpec((d, th), lambda i, j: (0, j)),
                  pl.BlockSpec((th, d), lambda i, j: (j, 0)),
                  pl.BlockSpec((1, d), lambda i, j: (0, 0)),
                  pl.BlockSpec((1, d), lambda i, j: (0, 5))],
        out_specs=pl.BlockSpec((tm, d), lambda i, j: (i, 0)),
        out_shape=jax.ShapeDtypeStruct((l, d), F32),
        scratch_shapes=[pltpu.VMEM((tm, d), BF16), pltpu.VMEM((tm, d), F32)],
        compiler_params=_params(("arbitrary", "arbitrary")),
        name="ffn_block",
    )(x, pre_w.reshape(1, d), mod, mod, w_gate, w_up, w_down, post_w.reshape(1, d), mod)


def _pad_cols(w, n):
    return jnp.pad(w, ((0, 0), (0, n - w.shape[1])))


def kernel(x, c, ada_w0, ada_b0, mix_pre0, mix_post0, ffn_pre0, ffn_post0, w_in0, s5_lambda_re, s5_lambda_im, s5_log_step, s5_b_re, s5_b_im, s5_c_re, s5_c_im, s5_d, s5_glu_w, s5_glu_b, gdn_conv_w, gdn_a_log, gdn_dt_bias, gdn_norm_w, w_out0, ffn_gate0, ffn_up0, ffn_down0, ada_w1, ada_b1, mix_pre1, mix_post1, ffn_pre1, ffn_post1, w_in1, gla_gate_w2, gla_gate_b, gla_norm_w, w_out1, ffn_gate1, ffn_up1, ffn_down1):
    bsz, l, d = x.shape
    assert bsz == 1
    x = x.reshape(l, d)
    tm = min(512, l)
    th = 512
    rb = min(512, l)
    cb = min(256, l // S5_T)
    n_hs = int(math.log2(cb))
    assert l % tm == 0 and l % rb == 0 and l % (cb * S5_T) == 0 and (1 << n_hs) == cb

    s5_w = s5_glu_w.shape[0]
    qk_w = (GDN_V_HEADS // 2) * GDN_HEAD_DIM
    v_w = GDN_V_HEADS * GDN_HEAD_DIM
    n0 = s5_w + 2 * qk_w + 2 * v_w
    dk_all = gla_gate_w2.shape[1]
    dv_all = w_out1.shape[0]
    n1 = 2 * dk_all + 2 * dv_all

    mod0 = _ada_modulation(c, ada_w0, ada_b0)
    mod1 = _ada_modulation(c, ada_w1, ada_b1)
    proj0, ab = _prenorm_project(x, mod0, 0, mix_pre0, w_in0[:, :n0].astype(BF16),
                                 _pad_cols(w_in0[:, n0:], LANES).astype(BF16), tm, 1024)
    tables = _s5_tables(s5_lambda_re, s5_lambda_im, s5_log_step, s5_b_re, s5_b_im, s5_c_re, s5_c_im, s5_d,
                        cb, n_hs)
    y_a = _s5_mixer(proj0, tables, l, cb, n_hs)
    y_b = _gdn_mixer(proj0, ab, gdn_conv_w, gdn_a_log, gdn_dt_bias, gdn_norm_w, l, rb)
    x = _out_project0(x, y_a, y_b, s5_glu_w.astype(BF16), s5_glu_b, w_out0.astype(BF16), mix_post0, mod0, tm)
    x = _ffn_block(x, mod0, ffn_pre0, ffn_post0, ffn_gate0.astype(BF16), ffn_up0.astype(BF16),
                   ffn_down0.astype(BF16), tm, th)

    proj1, g_low = _prenorm_project(x, mod1, 0, mix_pre1, w_in1[:, :n1].astype(BF16),
                                    _pad_cols(w_in1[:, n1:], LANES).astype(BF16), tm, 1024)
    y_c = _gla_mixer(proj1, g_low, gla_gate_w2, gla_gate_b, gla_norm_w, l, rb, dk_all, dv_all)
    x = _out_project1(x, y_c, w_out1.astype(BF16), mix_post1, mod1, tm)
    x = _ffn_block(x, mod1, ffn_pre1, ffn_post1, ffn_gate1.astype(BF16), ffn_up1.astype(BF16),
                   ffn_down1.astype(BF16), tm, th)
    return x.reshape(bsz, l, d)
```

```python
import functools
import math

import jax
import jax.numpy as jnp
from jax import lax
from jax.experimental import pallas as pl
from jax.experimental.pallas import tpu as pltpu

F32 = jnp.float32
BF16 = jnp.bfloat16
EPS = 1e-6

LANES = 128
MXU_N = 256
VMEM_LIMIT = 56 << 20

S5_GROUP = 16
S5_STATE = 64
S5_T = 16
GDN_HEAD_DIM = 128
GDN_V_HEADS = 8
GDN_CONV = 4
CHUNK = 64
INV_BASE = 16
PREP = 4
GLA_HEADS = 4
GLA_TAU = 16.0


def _params(sem):
    return pltpu.CompilerParams(dimension_semantics=sem, vmem_limit_bytes=VMEM_LIMIT)


def _dot(a, b):
    return jnp.dot(a, b, preferred_element_type=F32)


def _dot_nt(a, b):
    return lax.dot_general(a, b, (((1,), (1,)), ((), ())), preferred_element_type=F32)


def _dot_tn(a, b):
    return lax.dot_general(a, b, (((0,), (0,)), ((), ())), preferred_element_type=F32)


def _split_bf16(x):
    hi = x.astype(BF16)
    lo = (x - hi.astype(F32)).astype(BF16)
    return hi, lo


def _dot_split(a, b):
    ah, al = _split_bf16(a)
    bh, bl = _split_bf16(b)
    return _dot(ah, bh) + (_dot(ah, bl) + _dot(al, bh))


def _silu(x):
    return x * jax.nn.sigmoid(x)


def _prenorm(x, nw, sc, sh):
    ms = jnp.mean(x * x, axis=-1, keepdims=True)
    return (x * lax.rsqrt(ms + EPS) * nw) * (1.0 + sc) + sh


def _postnorm_residual(x, m, pw, gt):
    ms = jnp.mean(m * m, axis=-1, keepdims=True)
    return x + gt * (m * lax.rsqrt(ms + EPS) * pw)


def _shift_rows(x, s, row):
    return jnp.where(row >= s, pltpu.roll(x, s, 0), 0.0)


def _ada_kernel(c_ref, w_ref, b_ref, o_ref):
    s = _silu(c_ref[...])
    o_ref[...] = jnp.sum(s * w_ref[...], axis=0, keepdims=True) + b_ref[...]


def _ada_modulation(c, w, b):
    d, n = w.shape
    tn = 1024
    return pl.pallas_call(
        _ada_kernel,
        grid=(n // tn,),
        in_specs=[pl.BlockSpec((d, 1), lambda j: (0, 0)),
                  pl.BlockSpec((d, tn), lambda j: (0, j)),
                  pl.BlockSpec((1, tn), lambda j: (0, j))],
        out_specs=pl.BlockSpec((1, tn), lambda j: (0, j)),
        out_shape=jax.ShapeDtypeStruct((1, n), F32),
        compiler_params=_params(("arbitrary",)),
        name="ada_modulation",
    )(c.reshape(d, 1), w, b.reshape(1, n))


def _proj_kernel(x_ref, nw_ref, sc_ref, sh_ref, w_ref, ws_ref, o_ref, os_ref, h_ref):
    @pl.when(pl.program_id(1) == 0)
    def _():
        hb = _prenorm(x_ref[...], nw_ref[...], sc_ref[...], sh_ref[...]).astype(BF16)
        h_ref[...] = hb
        os_ref[...] = _dot(hb, ws_ref[...])

    o_ref[...] = _dot(h_ref[...], w_ref[...]).astype(o_ref.dtype)


def _prenorm_project(x, mod, mod_base, norm_w, w_main, w_small, tm, tn):
    l, d = x.shape
    n = w_main.shape[1]
    ns = w_small.shape[1]
    return pl.pallas_call(
        _proj_kernel,
        grid=(l // tm, n // tn),
        in_specs=[pl.BlockSpec((tm, d), lambda i, j: (i, 0)),
                  pl.BlockSpec((1, d), lambda i, j: (0, 0)),
                  pl.BlockSpec((1, d), lambda i, j: (0, mod_base + 1)),
                  pl.BlockSpec((1, d), lambda i, j: (0, mod_base)),
                  pl.BlockSpec((d, tn), lambda i, j: (0, j)),
                  pl.BlockSpec((d, ns), lambda i, j: (0, 0))],
        out_specs=[pl.BlockSpec((tm, tn), lambda i, j: (i, j)),
                   pl.BlockSpec((tm, ns), lambda i, j: (i, 0))],
        out_shape=[jax.ShapeDtypeStruct((l, n), BF16), jax.ShapeDtypeStruct((l, ns), F32)],
        scratch_shapes=[pltpu.VMEM((tm, d), BF16)],
        compiler_params=_params(("arbitrary", "arbitrary")),
        name="prenorm_project",
    )(x, norm_w.reshape(1, d), mod, mod, w_main, w_small)


def _s5_build_tables(b_ref, c_ref, lam_ref, d_ref, wi_ref, wb_ref, wc_ref, hs_ref, n_hs):
    t = S5_T
    half = b_ref.shape[2] // 2
    lr, li = lam_ref[0, 0:1, :], lam_ref[0, 1:2, :]
    dt = jnp.exp(lam_ref[0, 2:3, :])
    mag = jnp.exp(lr * dt)
    ar, ai = mag * jnp.cos(li * dt), mag * jnp.sin(li * dt)
    den = lr * lr + li * li
    nr, ni = ar - 1.0, ai
    f_re = (nr * lr + ni * li) / den
    f_im = (ni * lr - nr * li) / den
    b_re, b_im = b_ref[0, :, :half], b_ref[0, :, half:]
    bb_re = f_re * b_re - f_im * b_im
    bb_im = f_re * b_im + f_im * b_re
    c_re, c_im = c_ref[0, :, :half], c_ref[0, :, half:]
    bb_hi, bb_lo = _split_bf16(jnp.concatenate([bb_re, bb_im], axis=1))
    rr = lax.broadcasted_iota(jnp.int32, (LANES, LANES), 0)
    qq = lax.broadcasted_iota(jnp.int32, (LANES, LANES), 1)
    skip = jnp.where(rr == qq, d_ref[0], 0.0)

    pr, pi = jnp.ones_like(ar), jnp.zeros_like(ar)
    pows = []
    for d in range(t + 1):
        pows.append((pr, pi))
        cs = jnp.concatenate([c_re * pr - c_im * pi, -(c_re * pi + c_im * pr)], axis=1)
        if d < t:
            cs_hi, cs_lo = _split_bf16(cs)
            k_d = _dot_nt(bb_hi, cs_hi) + (_dot_nt(bb_hi, cs_lo) + _dot_nt(bb_lo, cs_hi))
            if d == 0:
                k_d = k_d + skip
            k_d = k_d.astype(BF16)
            for s in range(t - d):
                wi_ref[s * LANES:(s + 1) * LANES, (s + d) * LANES:(s + d + 1) * LANES] = k_d
        if d >= 1:
            wc_ref[:, (d - 1) * LANES:d * LANES] = cs.T.astype(BF16)
        pr, pi = pr * ar - pi * ai, pr * ai + pi * ar
    for tb in range(t * LANES // MXU_N):
        wi_ref[(2 * tb + 1) * LANES:(2 * tb + 2) * LANES, 2 * tb * LANES:(2 * tb + 1) * LANES] = (
            jnp.zeros((LANES, LANES), BF16))
    for s in range(t):
        pr, pi = pows[t - 1 - s]
        wb_ref[s * LANES:(s + 1) * LANES, :half] = (bb_re * pr - bb_im * pi).astype(BF16)
        wb_ref[s * LANES:(s + 1) * LANES, half:] = (bb_re * pi + bb_im * pr).astype(BF16)
    pr, pi = pows[t]
    for k in range(n_hs):
        hs_ref[k:k + 1, :half] = pr
        hs_ref[k:k + 1, half:] = pi
        pr, pi = pr * pr - pi * pi, 2.0 * pr * pi


def _s5_kernel(u_ref, b_ref, c_ref, lam_ref, d_ref, o_ref,
               wi_ref, wb_ref, wc_ref, hs_ref, uf_ref, u2_ref, yf_ref, carry_ref, *, cb, n_hs):
    t = S5_T
    half = carry_ref.shape[1] // 2

    @pl.when(pl.program_id(1) == 0)
    def _():
        carry_ref[...] = jnp.zeros_like(carry_ref)
        _s5_build_tables(b_ref, c_ref, lam_ref, d_ref, wi_ref, wb_ref, wc_ref, hs_ref, n_hs)

    uf_ref[...] = u_ref[...].astype(F32)
    for k in range(t):
        u2_ref[:, k * LANES:(k + 1) * LANES] = uf_ref[pl.ds(k, cb, stride=t), :].astype(BF16)
    u2 = u2_ref[...]

    s = _dot(u2, wb_ref[...])
    sr, si = s[:, :half], s[:, half:]
    row = lax.broadcasted_iota(jnp.int32, (cb, half), 0)
    cr, ci = carry_ref[:, :half], carry_ref[:, half:]
    pr, pi = hs_ref[0:1, :half], hs_ref[0:1, half:]
    sr = sr + jnp.where(row == 0, pr * cr - pi * ci, 0.0)
    si = si + jnp.where(row == 0, pr * ci + pi * cr, 0.0)
    for k in range(n_hs):
        pr, pi = hs_ref[k:k + 1, :half], hs_ref[k:k + 1, half:]
        shr, shi = _shift_rows(sr, 1 << k, row), _shift_rows(si, 1 << k, row)
        sr, si = sr + pr * shr - pi * shi, si + pr * shi + pi * shr
    epr = jnp.where(row >= 1, pltpu.roll(sr, 1, 0), cr)
    epi = jnp.where(row >= 1, pltpu.roll(si, 1, 0), ci)
    carry_ref[:, :half] = sr[cb - 1:cb]
    carry_ref[:, half:] = si[cb - 1:cb]
    ep = jnp.concatenate([epr, epi], axis=1).astype(BF16)
    y_inter = _dot(ep, wc_ref[...])

    for tb in range(t * LANES // MXU_N):
        kk = (tb + 1) * MXU_N
        yb = y_inter[:, tb * MXU_N:kk] + _dot(u2[:, :kk], wi_ref[:kk, tb * MXU_N:kk])
        yb = jax.nn.gelu(yb, approximate=True)
        for q in range(MXU_N // LANES):
            yf_ref[pl.ds(tb * (MXU_N // LANES) + q, cb, stride=t), :] = yb[:, q * LANES:(q + 1) * LANES]
    o_ref[...] = yf_ref[...].astype(o_ref.dtype)


def _s5_group_rows(x, nl):
    g, p, n = x.shape
    gl = g // nl
    x = jnp.tile(x.astype(F32).reshape(nl, gl * p, n), (1, 1, gl))
    rg = jnp.arange(gl * p)[:, None] // p
    cg = jnp.arange(gl * n)[None, :] // n
    return jnp.where(rg == cg, x, 0.0)


def _s5_mixer(proj, lam_re, lam_im, log_step, b_re, b_im, c_re, c_im, d_skip, l, cb, n_hs):
    t = S5_T
    g, n = lam_re.shape
    nl = g * S5_GROUP // LANES
    half = (g // nl) * n
    rb = cb * t
    brow = jnp.concatenate([_s5_group_rows(b_re.transpose(0, 2, 1), nl),
                            _s5_group_rows(b_im.transpose(0, 2, 1), nl)], axis=-1)
    crow = jnp.concatenate([_s5_group_rows(c_re, nl), _s5_group_rows(c_im, nl)], axis=-1)
    lam = jnp.stack([lam_re.astype(F32).reshape(nl, half), lam_im.astype(F32).reshape(nl, half),
                     jnp.repeat(log_step.astype(F32), n).reshape(nl, half)], axis=1)
    dsk = d_skip.astype(F32).reshape(nl, 1, LANES)
    kern = functools.partial(_s5_kernel, cb=cb, n_hs=n_hs)
    return pl.pallas_call(
        kern,
        grid=(nl, l // rb),
        in_specs=[pl.BlockSpec((rb, LANES), lambda a, b: (b, a)),
                  pl.BlockSpec((1, LANES, 2 * half), lambda a, b: (a, 0, 0)),
                  pl.BlockSpec((1, LANES, 2 * half), lambda a, b: (a, 0, 0)),
                  pl.BlockSpec((1, 3, half), lambda a, b: (a, 0, 0)),
                  pl.BlockSpec((1, 1, LANES), lambda a, b: (a, 0, 0))],
        out_specs=pl.BlockSpec((rb, LANES), lambda a, b: (b, a)),
        out_shape=jax.ShapeDtypeStruct((l, nl * LANES), BF16),
        scratch_shapes=[pltpu.VMEM((t * LANES, t * LANES), BF16),
                        pltpu.VMEM((t * LANES, 2 * half), BF16),
                        pltpu.VMEM((2 * half, t * LANES), BF16),
                        pltpu.VMEM((n_hs, 2 * half), F32),
                        pltpu.VMEM((rb, LANES), F32),
                        pltpu.VMEM((cb, t * LANES), BF16),
                        pltpu.VMEM((rb, LANES), F32),
                        pltpu.VMEM((1, 2 * half), F32)],
        compiler_params=_params(("arbitrary", "arbitrary")),
        name="s5_mixer",
    )(proj, brow, crow, lam, dsk)


def _unit_lower_inverse(lows):
    c = lows[0][0].shape[0]
    nb = len(lows[0])
    r = lax.broadcasted_iota(jnp.int32, (c, nb * c), 0)
    q = lax.broadcasted_iota(jnp.int32, (c, nb * c), 1)
    eye_w = jnp.where(q % c == r, 1.0, 0.0)
    rr = lax.broadcasted_iota(jnp.int32, (nb * c, nb * c), 0)
    qq = lax.broadcasted_iota(jnp.int32, (nb * c, nb * c), 1)
    on_diag = (rr // c) == (qq // c)

    def block_diag(xw):
        return jnp.where(on_diag, jnp.concatenate([xw] * nb, axis=0), 0.0).astype(BF16)

    def mm(xw, yw):
        return _dot(xw.astype(BF16), block_diag(yw))

    qc = q % c
    base = INV_BASE
    lws = [jnp.concatenate(group, axis=1) for group in lows]
    diag = [jnp.where(r // base == qc // base, lw, 0.0) for lw in lws]
    accs = [eye_w - d for d in diag]
    xs = [mm(d, d) for d in diag]
    steps = int(math.log2(base)) - 1
    for k in range(steps):
        last = k == steps - 1
        lhs = [acc if last else jnp.concatenate([x, acc], axis=0) for x, acc in zip(xs, accs)]
        both = [mm(a, x) for a, x in zip(lhs, xs)]
        accs = [acc + (b if last else b[c:]) for acc, b in zip(accs, both)]
        xs = [b[:c] for b in both]
    size = base
    while size < c:
        below = (r // (2 * size) == qc // (2 * size)) & (r // size != qc // size)
        subs = [jnp.where(below, lw, 0.0) for lw in lws]
        ys = [mm(s, acc) for s, acc in zip(subs, accs)]
        accs = [acc - mm(acc, y) for acc, y in zip(accs, ys)]
        size *= 2
    return [[acc[:, i * c:(i + 1) * c] for i in range(nb)] for acc in accs]


def _gdn_kernel(q_ref, k_ref, v_ref, z_ref, ab_ref, cw_ref, al_ref, dtb_ref, nw_ref, o_ref,
                xe_ref, qn_ref, kn_ref, vc_ref, gc_ref, beta_ref, s_ref,
                uv_ref, wk_ref, qd_ref, kd_ref, at_ref, gl_ref, *, rb):
    hd = GDN_HEAD_DIM
    nh = GDN_V_HEADS
    nqk = nh // 2
    qw = nqk * hd
    c = CHUNK
    halo = 8

    @pl.when(pl.program_id(0) == 0)
    def _():
        s_ref[...] = jnp.zeros_like(s_ref)
        xe_ref[0:halo, :] = jnp.zeros((halo, xe_ref.shape[1]), F32)

    xe_ref[halo:, 0:qw] = q_ref[...].astype(F32)
    xe_ref[halo:, qw:2 * qw] = k_ref[...].astype(F32)
    xe_ref[halo:, 2 * qw:] = v_ref[...].astype(F32)
    acc = cw_ref[GDN_CONV - 1:GDN_CONV, :] * xe_ref[halo:, :]
    for s in range(1, GDN_CONV):
        acc = acc + cw_ref[GDN_CONV - 1 - s:GDN_CONV - s, :] * xe_ref[halo - s:halo - s + rb, :]
    xe_ref[0:halo, :] = xe_ref[rb:rb + halo, :]
    qkv = _silu(acc)
    for p in range(nqk):
        qp = qkv[:, p * hd:(p + 1) * hd]
        qn_ref[:, p * hd:(p + 1) * hd] = qp * lax.rsqrt(jnp.sum(qp * qp, -1, keepdims=True) + EPS) * (hd ** -0.5)
        kp = qkv[:, qw + p * hd:qw + (p + 1) * hd]
        kn_ref[:, p * hd:(p + 1) * hd] = kp * lax.rsqrt(jnp.sum(kp * kp, -1, keepdims=True) + EPS)
    vc_ref[...] = qkv[:, 2 * qw:]

    ab = ab_ref[...]
    g = -jnp.exp(al_ref[...]) * jax.nn.softplus(ab + dtb_ref[...])
    beta_ref[...] = jax.nn.sigmoid(ab)
    row = lax.broadcasted_iota(jnp.int32, (rb, LANES), 0) % c
    for k in range(int(math.log2(c))):
        g = g + jnp.where(row >= (1 << k), pltpu.roll(g, 1 << k, 0), 0.0)
    gc_ref[...] = g

    ri = lax.broadcasted_iota(jnp.int32, (c, c), 0)
    ci = lax.broadcasted_iota(jnp.int32, (c, c), 1)
    causal = ri >= ci
    strict = ri > ci
    nw = nw_ref[...]

    def prepare_body(it, carry):
        rows = [pl.multiple_of((it * PREP + j) * c, c) for j in range(PREP)]
        scores = []
        for j, r0 in enumerate(rows):
            gcc = gc_ref[pl.ds(r0, c), :]
            gl_ref[it * PREP + j] = jnp.broadcast_to(jnp.exp(gcc[c - 1:c, :]), (8, LANES))
            for p in range(nqk):
                qp = qn_ref[pl.ds(r0, c), p * hd:(p + 1) * hd]
                kp = kn_ref[pl.ds(r0, c), p * hd:(p + 1) * hd]
                qk_kk = _dot_nt(jnp.concatenate([qp, kp], axis=0).astype(BF16), kp.astype(BF16))
                scores.append((r0, p, gcc, qp, kp, qk_kk))
        lows, rhss, where = [], [], []
        for r0, p, gcc, qp, kp, qk_kk in scores:
            gct = gcc.T
            bet = beta_ref[pl.ds(r0, c), :]
            for h in (2 * p, 2 * p + 1):
                cols = slice(h * hd, (h + 1) * hd)
                gcol = jnp.broadcast_to(gcc[:, h:h + 1], (c, hd))
                bcol = jnp.broadcast_to(bet[:, nh + h:nh + h + 1], (c, hd))
                dec = jnp.exp(jnp.where(causal, gcol[:, :c] - gct[h:h + 1, :], -jnp.inf))
                lows.append(jnp.where(strict, bcol[:, :c] * qk_kk[c:] * dec, 0.0))
                at_ref[pl.ds(r0, c), h * hd:h * hd + c] = (qk_kk[:c] * dec).astype(BF16)
                eg = jnp.exp(gcol)
                qd_ref[pl.ds(r0, c), cols] = (qp * eg).astype(BF16)
                kd_ref[pl.ds(r0, c), cols] = (kp * jnp.exp(gcc[c - 1:c, h:h + 1] - gcol)).astype(BF16)
                rhss.append(jnp.concatenate([vc_ref[pl.ds(r0, c), cols] * bcol, kp * (bcol * eg)],
                                            axis=1).astype(BF16))
                where.append((r0, cols))
        t_groups = _unit_lower_inverse([lows[i:i + 4] for i in range(0, len(lows), 4)])
        t_mats = [t for group in t_groups for t in group]
        uws = [_dot(t.astype(BF16), rhs) for t, rhs in zip(t_mats, rhss)]
        for uw, (r0, cols) in zip(uws, where):
            uv_ref[pl.ds(r0, c), cols] = uw[:, :hd]
            wk_ref[pl.ds(r0, c), cols] = uw[:, hd:].astype(BF16)
        return carry

    lax.fori_loop(0, rb // (PREP * c), prepare_body, 0)

    def state_body(n, carry):
        r0 = pl.multiple_of(n * c, c)
        gl = gl_ref[n]
        heads = [slice(h * hd, (h + 1) * hd) for h in range(nh)]
        sts = [s_ref[h] for h in range(nh)]
        ws_qs = [_dot(jnp.concatenate([wk_ref[pl.ds(r0, c), cols], qd_ref[pl.ds(r0, c), cols]], axis=0),
                      st.astype(BF16)) for cols, st in zip(heads, sts)]
        v16s = [(uv_ref[pl.ds(r0, c), cols] - wq[:c]).astype(BF16) for cols, wq in zip(heads, ws_qs)]
        upd = [_dot_tn(kd_ref[pl.ds(r0, c), cols], v16) for cols, v16 in zip(heads, v16s)]
        intra = [_dot(at_ref[pl.ds(r0, c), h * hd:h * hd + c], v16) for h, v16 in enumerate(v16s)]
        for h, cols in enumerate(heads):
            s_ref[h] = sts[h] * gl[0:1, h:h + 1] + upd[h]
            o = ws_qs[h][c:] + intra[h]
            zh = z_ref[pl.ds(r0, c), cols].astype(F32)
            on = o * lax.rsqrt(jnp.mean(o * o, -1, keepdims=True) + EPS) * nw
            o_ref[pl.ds(r0, c), cols] = (on * _silu(zh)).astype(o_ref.dtype)
        return carry

    lax.fori_loop(0, rb // c, state_body, 0)


def _gdn_mixer(proj, ab, conv_w, a_log, dt_bias, norm_w, l, rb):
    hd, nh = GDN_HEAD_DIM, GDN_V_HEADS
    qw = (nh // 2) * hd
    vw = nh * hd
    pad = LANES - nh
    al = jnp.pad(a_log.astype(F32), (0, pad)).reshape(1, LANES)
    dtb = jnp.pad(dt_bias.astype(F32), (0, pad)).reshape(1, LANES)
    kern = functools.partial(_gdn_kernel, rb=rb)
    return pl.pallas_call(
        kern,
        grid=(l // rb,),
        in_specs=[pl.BlockSpec((rb, qw), lambda i: (i, 2)),
                  pl.BlockSpec((rb, qw), lambda i: (i, 3)),
                  pl.BlockSpec((rb, vw), lambda i: (i, 2)),
                  pl.BlockSpec((rb, vw), lambda i: (i, 3)),
                  pl.BlockSpec((rb, LANES), lambda i: (i, 0)),
                  pl.BlockSpec((GDN_CONV, 2 * qw + vw), lambda i: (0, 0)),
                  pl.BlockSpec((1, LANES), lambda i: (0, 0)),
                  pl.BlockSpec((1, LANES), lambda i: (0, 0)),
                  pl.BlockSpec((1, hd), lambda i: (0, 0))],
        out_specs=pl.BlockSpec((rb, vw), lambda i: (i, 0)),
        out_shape=jax.ShapeDtypeStruct((l, vw), BF16),
        scratch_shapes=[pltpu.VMEM((rb + 8, 2 * qw + vw), F32),
                        pltpu.VMEM((rb, qw), F32),
                        pltpu.VMEM((rb, qw), F32),
                        pltpu.VMEM((rb, vw), F32),
                        pltpu.VMEM((rb, LANES), F32),
                        pltpu.VMEM((rb, LANES), F32),
                        pltpu.VMEM((nh, hd, hd), F32),
                        pltpu.VMEM((rb, vw), F32),
                        pltpu.VMEM((rb, vw), BF16),
                        pltpu.VMEM((rb, vw), BF16),
                        pltpu.VMEM((rb, vw), BF16),
                        pltpu.VMEM((rb, vw), BF16),
                        pltpu.VMEM((rb // CHUNK, 8, LANES), F32)],
        compiler_params=_params(("arbitrary",)),
        name="gdn_mixer",
    )(proj, proj, proj, proj, ab, conv_w.astype(F32), al, dtb, norm_w.astype(F32).reshape(1, hd))


def _gla_kernel(q_ref, k_ref, v_ref, r_ref, gl_ref, w2_ref, gb_ref, nw_ref, o_ref,
                qt_ref, kt_ref, kd_ref, gle_ref, st_ref, *, rb):
    nh = GLA_HEADS
    dk = q_ref.shape[1] // nh
    dv = v_ref.shape[1] // nh
    c = CHUNK

    @pl.when(pl.program_id(0) == 0)
    def _():
        st_ref[...] = jnp.zeros_like(st_ref)

    x = _dot_split(gl_ref[...], w2_ref[...]) + gb_ref[...]
    b = jax.nn.log_sigmoid(x) / GLA_TAU
    row = lax.broadcasted_iota(jnp.int32, b.shape, 0) % c
    for k in range(int(math.log2(c))):
        b = b + jnp.where(row >= (1 << k), pltpu.roll(b, 1 << k, 0), 0.0)
    q = q_ref[...].astype(F32) * (dk ** -0.5)
    kf = k_ref[...].astype(F32)
    qt_ref[...] = (q * jnp.exp(b)).astype(BF16)
    kt_ref[...] = (kf * jnp.exp(-b)).astype(BF16)
    for n in range(rb // c):
        b_last = b[(n + 1) * c - 1:(n + 1) * c, :]
        kd_ref[n * c:(n + 1) * c, :] = (kf[n * c:(n + 1) * c] * jnp.exp(b_last - b[n * c:(n + 1) * c])).astype(BF16)
        gle_ref[n] = jnp.broadcast_to(jnp.exp(b_last), (8, b.shape[1]))

    ri = lax.broadcasted_iota(jnp.int32, (c, c), 0)
    ci = lax.broadcasted_iota(jnp.int32, (c, c), 1)
    causal = ri >= ci
    nw = nw_ref[...]

    def chunk_body(n, carry):
        r0 = pl.multiple_of(n * c, c)
        gle = gle_ref[n]
        qts = [qt_ref[pl.ds(r0, c), h * dk:(h + 1) * dk] for h in range(nh)]
        vhs = [v_ref[pl.ds(r0, c), h * dv:(h + 1) * dv] for h in range(nh)]
        sts = [st_ref[h] for h in range(nh)]
        scores = [_dot_nt(qts[h], kt_ref[pl.ds(r0, c), h * dk:(h + 1) * dk]) for h in range(nh)]
        inter = [_dot_nt(qts[h], sts[h].astype(BF16)) for h in range(nh)]
        upd = [_dot_tn(vhs[h], kd_ref[pl.ds(r0, c), h * dk:(h + 1) * dk]) for h in range(nh)]
        intra = [_dot(jnp.where(causal, scores[h], 0.0).astype(BF16), vhs[h]) for h in range(nh)]
        for h in range(nh):
            st_ref[h] = sts[h] * gle[0:1, h * dk:(h + 1) * dk] + upd[h]
            o = intra[h] + inter[h]
            rh = r_ref[pl.ds(r0, c), h * dv:(h + 1) * dv].astype(F32)
            on = o * lax.rsqrt(jnp.mean(o * o, -1, keepdims=True) + EPS) * nw
            o_ref[pl.ds(r0, c), h * dv:(h + 1) * dv] = (on * _silu(rh)).astype(o_ref.dtype)
        return carry

    lax.fori_loop(0, rb // c, chunk_body, 0)


def _gla_mixer(proj, g_low, gate_w2, gate_b, norm_w, l, rb, dk_all, dv_all):
    nh = GLA_HEADS
    lowrank = gate_w2.shape[0]
    w2 = jnp.pad(gate_w2.astype(F32), ((0, LANES - lowrank), (0, 0)))
    kern = functools.partial(_gla_kernel, rb=rb)
    return pl.pallas_call(
        kern,
        grid=(l // rb,),
        in_specs=[pl.BlockSpec((rb, dk_all), lambda i: (i, 0)),
                  pl.BlockSpec((rb, dk_all), lambda i: (i, 1)),
                  pl.BlockSpec((rb, dv_all), lambda i: (i, 1)),
                  pl.BlockSpec((rb, dv_all), lambda i: (i, 2)),
                  pl.BlockSpec((rb, LANES), lambda i: (i, 0)),
                  pl.BlockSpec((LANES, dk_all), lambda i: (0, 0)),
                  pl.BlockSpec((1, dk_all), lambda i: (0, 0)),
                  pl.BlockSpec((1, dv_all // nh), lambda i: (0, 0))],
        out_specs=pl.BlockSpec((rb, dv_all), lambda i: (i, 0)),
        out_shape=jax.ShapeDtypeStruct((l, dv_all), BF16),
        scratch_shapes=[pltpu.VMEM((rb, dk_all), BF16),
                        pltpu.VMEM((rb, dk_all), BF16),
                        pltpu.VMEM((rb, dk_all), BF16),
                        pltpu.VMEM((rb // CHUNK, 8, dk_all), F32),
                        pltpu.VMEM((nh, dv_all // nh, dk_all // nh), F32)],
        compiler_params=_params(("arbitrary",)),
        name="gla_mixer",
    )(proj, proj, proj, proj, g_low, w2, gate_b.astype(F32).reshape(1, dk_all),
      norm_w.astype(F32).reshape(1, dv_all // nh))


def _out0_kernel(x_ref, ya_ref, yb_ref, gw_ref, gb_ref, wa_ref, wb_ref, pw_ref, gt_ref, o_ref):
    ya = ya_ref[...]
    gate = jax.nn.sigmoid(_dot(ya, gw_ref[...]) + gb_ref[...])
    ya = (ya.astype(F32) * gate).astype(BF16)
    m = _dot(ya, wa_ref[...]) + _dot(yb_ref[...], wb_ref[...])
    o_ref[...] = _postnorm_residual(x_ref[...], m, pw_ref[...], gt_ref[...])


def _out1_kernel(x_ref, y_ref, w_ref, pw_ref, gt_ref, o_ref):
    m = _dot(y_ref[...], w_ref[...])
    o_ref[...] = _postnorm_residual(x_ref[...], m, pw_ref[...], gt_ref[...])


def _const_spec(shape):
    return pl.BlockSpec(shape, lambda i: (0,) * len(shape), pipeline_mode=pl.Buffered(1))


def _out_project0(x, ya, yb, glu_w, glu_b, w_out, post_w, mod, tm):
    l, d = x.shape
    wa = ya.shape[1]
    wb = yb.shape[1]
    return pl.pallas_call(
        _out0_kernel,
        grid=(l // tm,),
        in_specs=[pl.BlockSpec((tm, d), lambda i: (i, 0)),
                  pl.BlockSpec((tm, wa), lambda i: (i, 0)),
                  pl.BlockSpec((tm, wb), lambda i: (i, 0)),
                  _const_spec((wa, wa)),
                  _const_spec((1, wa)),
                  pl.BlockSpec((wa, d), lambda i: (0, 0), pipeline_mode=pl.Buffered(1)),
                  pl.BlockSpec((wb, d), lambda i: (1, 0), pipeline_mode=pl.Buffered(1)),
                  _const_spec((1, d)),
                  pl.BlockSpec((1, d), lambda i: (0, 2))],
        out_specs=pl.BlockSpec((tm, d), lambda i: (i, 0)),
        out_shape=jax.ShapeDtypeStruct((l, d), F32),
        compiler_params=_params(("arbitrary",)),
        name="out_project0",
    )(x, ya, yb, glu_w, glu_b.astype(F32).reshape(1, wa), w_out, w_out, post_w.reshape(1, d), mod)


def _out_project1(x, y, w_out, post_w, mod, tm):
    l, d = x.shape
    w = y.shape[1]
    return pl.pallas_call(
        _out1_kernel,
        grid=(l // tm,),
        in_specs=[pl.BlockSpec((tm, d), lambda i: (i, 0)),
                  pl.BlockSpec((tm, w), lambda i: (i, 0)),
                  _const_spec((w, d)),
                  _const_spec((1, d)),
                  pl.BlockSpec((1, d), lambda i: (0, 2))],
        out_specs=pl.BlockSpec((tm, d), lambda i: (i, 0)),
        out_shape=jax.ShapeDtypeStruct((l, d), F32),
        compiler_params=_params(("arbitrary",)),
        name="out_project1",
    )(x, y, w_out, post_w.reshape(1, d), mod)


def _ffn_kernel(x_ref, nw_ref, sc_ref, sh_ref, wg_ref, wu_ref, wd_ref, pw_ref, gt_ref, o_ref,
                h_ref, a_ref, m_ref, *, n_hid, n_out):
    j = pl.program_id(1)
    th = a_ref.shape[2]
    tn = wd_ref.shape[1]

    @pl.when(j == 0)
    def _():
        h_ref[...] = _prenorm(x_ref[...], nw_ref[...], sc_ref[...], sh_ref[...]).astype(BF16)

    @pl.when(j < n_hid)
    def _():
        h = h_ref[...]
        a_ref[j] = (_silu(_dot(h, wg_ref[...])) * _dot(h, wu_ref[...])).astype(BF16)

    for jj in range(n_out):
        @pl.when(j == n_hid + jj)
        def _():
            acc = _dot(a_ref[0], wd_ref[0:th, :])
            for k in range(1, n_hid):
                acc = acc + _dot(a_ref[k], wd_ref[k * th:(k + 1) * th, :])
            m_ref[:, jj * tn:(jj + 1) * tn] = acc

    @pl.when(j == n_hid + n_out - 1)
    def _():
        o_ref[...] = _postnorm_residual(x_ref[...], m_ref[...], pw_ref[...], gt_ref[...])


def _ffn_block(x, mod, pre_w, post_w, w_gate, w_up, w_down, tm, th):
    l, d = x.shape
    hid = w_gate.shape[1]
    tn = th
    n_hid, n_out = hid // th, d // tn
    kern = functools.partial(_ffn_kernel, n_hid=n_hid, n_out=n_out)
    return pl.pallas_call(
        kern,
        grid=(l // tm, n_hid + n_out),
        in_specs=[pl.BlockSpec((tm, d), lambda i, j: (i, 0)),
                  pl.BlockSpec((1, d), lambda i, j: (0, 0)),
                  pl.BlockSpec((1, d), lambda i, j: (0, 4)),
                  pl.BlockSpec((1, d), lambda i, j: (0, 3)),
                  pl.BlockSpec((d, th), lambda i, j: (0, jnp.minimum(j, n_hid - 1))),
                  pl.BlockSpec((d, th), lambda i, j: (0, jnp.minimum(j, n_hid - 1))),
                  pl.BlockSpec((hid, tn), lambda i, j: (0, jnp.maximum(j - n_hid, 0))),
                  pl.BlockSpec((1, d), lambda i, j: (0, 0)),
                  pl.BlockSpec((1, d), lambda i, j: (0, 5))],
        out_specs=pl.BlockSpec((tm, d), lambda i, j: (i, 0)),
        out_shape=jax.ShapeDtypeStruct((l, d), F32),
        scratch_shapes=[pltpu.VMEM((tm, d), BF16),
                        pltpu.VMEM((n_hid, tm, th), BF16),
                        pltpu.VMEM((tm, d), F32)],
        compiler_params=_params(("arbitrary", "arbitrary")),
        name="ffn_block",
    )(x, pre_w.reshape(1, d), mod, mod, w_gate, w_up, w_down, post_w.reshape(1, d), mod)


def _pad_cols(w, n):
    return jnp.pad(w, ((0, 0), (0, n - w.shape[1])))


def kernel(x, c, ada_w0, ada_b0, mix_pre0, mix_post0, ffn_pre0, ffn_post0, w_in0, s5_lambda_re, s5_lambda_im, s5_log_step, s5_b_re, s5_b_im, s5_c_re, s5_c_im, s5_d, s5_glu_w, s5_glu_b, gdn_conv_w, gdn_a_log, gdn_dt_bias, gdn_norm_w, w_out0, ffn_gate0, ffn_up0, ffn_down0, ada_w1, ada_b1, mix_pre1, mix_post1, ffn_pre1, ffn_post1, w_in1, gla_gate_w2, gla_gate_b, gla_norm_w, w_out1, ffn_gate1, ffn_up1, ffn_down1):
    bsz, l, d = x.shape
    assert bsz == 1
    x = x.reshape(l, d)
    tm = min(512, l)
    tp = min(1024, l)
    th = 512
    rb = min(512, l)
    cb = min(512, l // S5_T)
    n_hs = int(math.log2(cb))
    assert l % tm == 0 and l % rb == 0 and l % (cb * S5_T) == 0 and (1 << n_hs) == cb

    s5_w = s5_glu_w.shape[0]
    qk_w = (GDN_V_HEADS // 2) * GDN_HEAD_DIM
    v_w = GDN_V_HEADS * GDN_HEAD_DIM
    n0 = s5_w + 2 * qk_w + 2 * v_w
    dk_all = gla_gate_w2.shape[1]
    dv_all = w_out1.shape[0]
    n1 = 2 * dk_all + 2 * dv_all

    mod0 = _ada_modulation(c, ada_w0, ada_b0)
    mod1 = _ada_modulation(c, ada_w1, ada_b1)
    proj0, ab = _prenorm_project(x, mod0, 0, mix_pre0, w_in0[:, :n0].astype(BF16),
                                 _pad_cols(w_in0[:, n0:], LANES).astype(BF16), tp, 1024)
    y_a = _s5_mixer(proj0, s5_lambda_re, s5_lambda_im, s5_log_step, s5_b_re, s5_b_im, s5_c_re, s5_c_im, s5_d,
                    l, cb, n_hs)
    y_b = _gdn_mixer(proj0, ab, gdn_conv_w, gdn_a_log, gdn_dt_bias, gdn_norm_w, l, rb)
    x = _out_project0(x, y_a, y_b, s5_glu_w.astype(BF16), s5_glu_b, w_out0.astype(BF16), mix_post0, mod0, tm)
    x = _ffn_block(x, mod0, ffn_pre0, ffn_post0, ffn_gate0.astype(BF16), ffn_up0.astype(BF16),
                   ffn_down0.astype(BF16), tm, th)

    proj1, g_low = _prenorm_project(x, mod1, 0, mix_pre1, w_in1[:, :n1].astype(BF16),
                                    _pad_cols(w_in1[:, n1:], LANES).astype(BF16), tp, 1024)
    y_c = _gla_mixer(proj1, g_low, gla_gate_w2, gla_gate_b, gla_norm_w, l, rb, dk_all, dv_all)
    x = _out_project1(x, y_c, w_out1.astype(BF16), mix_post1, mod1, tm)
    x = _ffn_block(x, mod1, ffn_pre1, ffn_post1, ffn_gate1.astype(BF16), ffn_up1.astype(BF16),
                   ffn_down1.astype(BF16), tm, th)
    return x.reshape(bsz, l, d)
```

```python
import functools
import math

import jax
import jax.numpy as jnp
from jax import lax
from jax.experimental import pallas as pl
from jax.experimental.pallas import tpu as pltpu

F32 = jnp.float32
BF16 = jnp.bfloat16
EPS = 1e-6

LANES = 128
MXU_N = 256
VMEM_LIMIT = 56 << 20

S5_GROUP = 16
S5_STATE = 64
S5_T = 16
GDN_HEAD_DIM = 128
GDN_V_HEADS = 8
GDN_CONV = 4
CHUNK = 64
INV_BASE = 16
PREP = 8
GLA_HEADS = 4
GLA_TAU = 16.0


def _params(sem):
    return pltpu.CompilerParams(dimension_semantics=sem, vmem_limit_bytes=VMEM_LIMIT)


def _dot(a, b):
    return jnp.dot(a, b, preferred_element_type=F32)


def _dot_nt(a, b):
    return lax.dot_general(a, b, (((1,), (1,)), ((), ())), preferred_element_type=F32)


def _dot_tn(a, b):
    return lax.dot_general(a, b, (((0,), (0,)), ((), ())), preferred_element_type=F32)


def _split_bf16(x):
    hi = x.astype(BF16)
    lo = (x - hi.astype(F32)).astype(BF16)
    return hi, lo


def _dot_split(a, b):
    ah, al = _split_bf16(a)
    bh, bl = _split_bf16(b)
    return _dot(ah, bh) + (_dot(ah, bl) + _dot(al, bh))


def _silu(x):
    return x * jax.nn.sigmoid(x)


def _prenorm(x, nw, sc, sh):
    ms = jnp.mean(x * x, axis=-1, keepdims=True)
    return (x * lax.rsqrt(ms + EPS)) * (nw * (1.0 + sc)) + sh


def _postnorm_residual(x, m, pw, gt):
    ms = jnp.mean(m * m, axis=-1, keepdims=True)
    return x + (m * lax.rsqrt(ms + EPS)) * (gt * pw)


def _shift_rows(x, s, row):
    return jnp.where(row >= s, pltpu.roll(x, s, 0), 0.0)


def _ada_kernel(c_ref, w_ref, b_ref, o_ref):
    s = _silu(c_ref[...])
    o_ref[...] = jnp.sum(s * w_ref[...], axis=0, keepdims=True) + b_ref[...]


def _ada_modulation(c, w, b):
    d, n = w.shape
    tn = 1024
    return pl.pallas_call(
        _ada_kernel,
        grid=(n // tn,),
        in_specs=[pl.BlockSpec((d, 1), lambda j: (0, 0)),
                  pl.BlockSpec((d, tn), lambda j: (0, j)),
                  pl.BlockSpec((1, tn), lambda j: (0, j))],
        out_specs=pl.BlockSpec((1, tn), lambda j: (0, j)),
        out_shape=jax.ShapeDtypeStruct((1, n), F32),
        compiler_params=_params(("arbitrary",)),
        name="ada_modulation",
    )(c.reshape(d, 1), w, b.reshape(1, n))


def _proj_kernel(x_ref, nw_ref, sc_ref, sh_ref, w_ref, ws_ref, o_ref, os_ref, h_ref):
    @pl.when(pl.program_id(1) == 0)
    def _():
        hb = _prenorm(x_ref[...], nw_ref[...], sc_ref[...], sh_ref[...]).astype(BF16)
        h_ref[...] = hb
        os_ref[...] = _dot(hb, ws_ref[...])

    o_ref[...] = _dot(h_ref[...], w_ref[...]).astype(o_ref.dtype)


def _prenorm_project(x, mod, mod_base, norm_w, w_all, n, tm, tn):
    l, d = x.shape
    w_main = w_all.astype(BF16)
    w_small = jnp.pad(w_all[:, n:], ((0, 0), (0, LANES - (w_all.shape[1] - n)))).astype(BF16)
    ns = LANES
    return pl.pallas_call(
        _proj_kernel,
        grid=(l // tm, n // tn),
        in_specs=[pl.BlockSpec((tm, d), lambda i, j: (i, 0)),
                  pl.BlockSpec((1, d), lambda i, j: (0, 0)),
                  pl.BlockSpec((1, d), lambda i, j: (0, mod_base + 1)),
                  pl.BlockSpec((1, d), lambda i, j: (0, mod_base)),
                  pl.BlockSpec((d, tn), lambda i, j: (0, j)),
                  pl.BlockSpec((d, ns), lambda i, j: (0, 0))],
        out_specs=[pl.BlockSpec((tm, tn), lambda i, j: (i, j)),
                   pl.BlockSpec((tm, ns), lambda i, j: (i, 0))],
        out_shape=[jax.ShapeDtypeStruct((l, n), BF16), jax.ShapeDtypeStruct((l, ns), F32)],
        scratch_shapes=[pltpu.VMEM((tm, d), BF16)],
        compiler_params=_params(("arbitrary", "arbitrary")),
        name="prenorm_project",
    )(x, norm_w.reshape(1, d), mod, mod, w_main, w_small)


def _s5_build_tables(b_ref, c_ref, lam_ref, d_ref, wi_ref, wb_ref, wc_ref, hs_ref, n_hs):
    t = S5_T
    half = b_ref.shape[2] // 2
    lr, li = lam_ref[0, 0:1, :], lam_ref[0, 1:2, :]
    dt = jnp.exp(lam_ref[0, 2:3, :])
    mag = jnp.exp(lr * dt)
    ar, ai = mag * jnp.cos(li * dt), mag * jnp.sin(li * dt)
    den = lr * lr + li * li
    nr, ni = ar - 1.0, ai
    f_re = (nr * lr + ni * li) / den
    f_im = (ni * lr - nr * li) / den
    b_re, b_im = b_ref[0, :, :half], b_ref[0, :, half:]
    bb_re = f_re * b_re - f_im * b_im
    bb_im = f_re * b_im + f_im * b_re
    c_re, c_im = c_ref[0, :, :half], c_ref[0, :, half:]
    bb_hi, bb_lo = _split_bf16(jnp.concatenate([bb_re, bb_im], axis=1))
    rr = lax.broadcasted_iota(jnp.int32, (LANES, LANES), 0)
    qq = lax.broadcasted_iota(jnp.int32, (LANES, LANES), 1)
    skip = jnp.where(rr == qq, d_ref[0], 0.0)

    pr, pi = jnp.ones_like(ar), jnp.zeros_like(ar)
    pows = []
    for d in range(t + 1):
        pows.append((pr, pi))
        cs = jnp.concatenate([c_re * pr - c_im * pi, -(c_re * pi + c_im * pr)], axis=1)
        if d < t:
            cs_hi, cs_lo = _split_bf16(cs)
            k_d = _dot_nt(bb_hi, cs_hi) + (_dot_nt(bb_hi, cs_lo) + _dot_nt(bb_lo, cs_hi))
            if d == 0:
                k_d = k_d + skip
            k_d = k_d.astype(BF16)
            for s in range(t - d):
                wi_ref[s * LANES:(s + 1) * LANES, (s + d) * LANES:(s + d + 1) * LANES] = k_d
        if d >= 1:
            wc_ref[:, (d - 1) * LANES:d * LANES] = cs.T.astype(BF16)
        pr, pi = pr * ar - pi * ai, pr * ai + pi * ar
    for tb in range(t * LANES // MXU_N):
        wi_ref[(2 * tb + 1) * LANES:(2 * tb + 2) * LANES, 2 * tb * LANES:(2 * tb + 1) * LANES] = (
            jnp.zeros((LANES, LANES), BF16))
    for s in range(t):
        pr, pi = pows[t - 1 - s]
        wb_ref[s * LANES:(s + 1) * LANES, :half] = (bb_re * pr - bb_im * pi).astype(BF16)
        wb_ref[s * LANES:(s + 1) * LANES, half:] = (bb_re * pi + bb_im * pr).astype(BF16)
    pr, pi = pows[t]
    for k in range(n_hs):
        hs_ref[k:k + 1, :half] = pr
        hs_ref[k:k + 1, half:] = pi
        pr, pi = pr * pr - pi * pi, 2.0 * pr * pi


def _s5_kernel(u_ref, b_ref, c_ref, lam_ref, d_ref, o_ref,
               wi_ref, wb_ref, wc_ref, hs_ref, uf_ref, u2_ref, yf_ref, carry_ref, *, cb, n_hs):
    t = S5_T
    half = carry_ref.shape[1] // 2

    @pl.when(pl.program_id(1) == 0)
    def _():
        carry_ref[...] = jnp.zeros_like(carry_ref)
        _s5_build_tables(b_ref, c_ref, lam_ref, d_ref, wi_ref, wb_ref, wc_ref, hs_ref, n_hs)

    uf_ref[...] = u_ref[...].astype(F32)
    for k in range(t):
        u2_ref[:, k * LANES:(k + 1) * LANES] = uf_ref[pl.ds(k, cb, stride=t), :].astype(BF16)
    u2 = u2_ref[...]

    s = _dot(u2, wb_ref[...])
    sr, si = s[:, :half], s[:, half:]
    row = lax.broadcasted_iota(jnp.int32, (cb, half), 0)
    cr, ci = carry_ref[:, :half], carry_ref[:, half:]
    pr, pi = hs_ref[0:1, :half], hs_ref[0:1, half:]
    sr = sr + jnp.where(row == 0, pr * cr - pi * ci, 0.0)
    si = si + jnp.where(row == 0, pr * ci + pi * cr, 0.0)
    for k in range(n_hs):
        pr, pi = hs_ref[k:k + 1, :half], hs_ref[k:k + 1, half:]
        shr, shi = _shift_rows(sr, 1 << k, row), _shift_rows(si, 1 << k, row)
        sr, si = sr + pr * shr - pi * shi, si + pr * shi + pi * shr
    epr = jnp.where(row >= 1, pltpu.roll(sr, 1, 0), cr)
    epi = jnp.where(row >= 1, pltpu.roll(si, 1, 0), ci)
    carry_ref[:, :half] = sr[cb - 1:cb]
    carry_ref[:, half:] = si[cb - 1:cb]
    ep = jnp.concatenate([epr, epi], axis=1).astype(BF16)
    y_inter = _dot(ep, wc_ref[...])

    for tb in range(t * LANES // MXU_N):
        kk = (tb + 1) * MXU_N
        yb = y_inter[:, tb * MXU_N:kk] + _dot(u2[:, :kk], wi_ref[:kk, tb * MXU_N:kk])
        yb = jax.nn.gelu(yb, approximate=True)
        for q in range(MXU_N // LANES):
            yf_ref[pl.ds(tb * (MXU_N // LANES) + q, cb, stride=t), :] = yb[:, q * LANES:(q + 1) * LANES]
    o_ref[...] = yf_ref[...].astype(o_ref.dtype)


def _s5_group_rows(x, nl):
    g, p, n = x.shape
    gl = g // nl
    x = jnp.tile(x.astype(F32).reshape(nl, gl * p, n), (1, 1, gl))
    rg = jnp.arange(gl * p)[:, None] // p
    cg = jnp.arange(gl * n)[None, :] // n
    return jnp.where(rg == cg, x, 0.0)


def _s5_mixer(proj, lam_re, lam_im, log_step, b_re, b_im, c_re, c_im, d_skip, l, cb, n_hs):
    t = S5_T
    g, n = lam_re.shape
    nl = g * S5_GROUP // LANES
    half = (g // nl) * n
    rb = cb * t
    brow = jnp.concatenate([_s5_group_rows(b_re.transpose(0, 2, 1), nl),
                            _s5_group_rows(b_im.transpose(0, 2, 1), nl)], axis=-1)
    crow = jnp.concatenate([_s5_group_rows(c_re, nl), _s5_group_rows(c_im, nl)], axis=-1)
    lam = jnp.stack([lam_re.astype(F32).reshape(nl, half), lam_im.astype(F32).reshape(nl, half),
                     jnp.repeat(log_step.astype(F32), n).reshape(nl, half)], axis=1)
    dsk = d_skip.astype(F32).reshape(nl, 1, LANES)
    kern = functools.partial(_s5_kernel, cb=cb, n_hs=n_hs)
    return pl.pallas_call(
        kern,
        grid=(nl, l // rb),
        in_specs=[pl.BlockSpec((rb, LANES), lambda a, b: (b, a)),
                  pl.BlockSpec((1, LANES, 2 * half), lambda a, b: (a, 0, 0)),
                  pl.BlockSpec((1, LANES, 2 * half), lambda a, b: (a, 0, 0)),
                  pl.BlockSpec((1, 3, half), lambda a, b: (a, 0, 0)),
                  pl.BlockSpec((1, 1, LANES), lambda a, b: (a, 0, 0))],
        out_specs=pl.BlockSpec((rb, LANES), lambda a, b: (b, a)),
        out_shape=jax.ShapeDtypeStruct((l, nl * LANES), BF16),
        scratch_shapes=[pltpu.VMEM((t * LANES, t * LANES), BF16),
                        pltpu.VMEM((t * LANES, 2 * half), BF16),
                        pltpu.VMEM((2 * half, t * LANES), BF16),
                        pltpu.VMEM((n_hs, 2 * half), F32),
                        pltpu.VMEM((rb, LANES), F32),
                        pltpu.VMEM((cb, t * LANES), BF16),
                        pltpu.VMEM((rb, LANES), F32),
                        pltpu.VMEM((1, 2 * half), F32)],
        compiler_params=_params(("arbitrary", "arbitrary")),
        name="s5_mixer",
    )(proj, brow, crow, lam, dsk)


def _unit_lower_inverse(lows):
    c = lows[0][0].shape[0]
    nb = len(lows[0])
    r = lax.broadcasted_iota(jnp.int32, (c, nb * c), 0)
    q = lax.broadcasted_iota(jnp.int32, (c, nb * c), 1)
    eye_w = jnp.where(q % c == r, 1.0, 0.0)
    rr = lax.broadcasted_iota(jnp.int32, (nb * c, nb * c), 0)
    qq = lax.broadcasted_iota(jnp.int32, (nb * c, nb * c), 1)
    on_diag = (rr // c) == (qq // c)

    def block_diag(xw):
        return jnp.where(on_diag, jnp.concatenate([xw] * nb, axis=0), 0.0).astype(BF16)

    def mm(xw, yw):
        return _dot(xw.astype(BF16), block_diag(yw))

    qc = q % c
    base = INV_BASE
    lws = [jnp.concatenate(group, axis=1) for group in lows]
    diag = [jnp.where(r // base == qc // base, lw, 0.0) for lw in lws]
    accs = [eye_w - d for d in diag]
    xs = [mm(d, d) for d in diag]
    steps = int(math.log2(base)) - 1
    for k in range(steps):
        last = k == steps - 1
        lhs = [acc if last else jnp.concatenate([x, acc], axis=0) for x, acc in zip(xs, accs)]
        both = [mm(a, x) for a, x in zip(lhs, xs)]
        accs = [acc + (b if last else b[c:]) for acc, b in zip(accs, both)]
        xs = [b[:c] for b in both]
    size = base
    while size < c:
        below = (r // (2 * size) == qc // (2 * size)) & (r // size != qc // size)
        subs = [jnp.where(below, lw, 0.0) for lw in lws]
        ys = [mm(s, acc) for s, acc in zip(subs, accs)]
        accs = [acc - mm(acc, y) for acc, y in zip(accs, ys)]
        size *= 2
    return [[acc[:, i * c:(i + 1) * c] for i in range(nb)] for acc in accs]


def _gdn_kernel(q_ref, k_ref, v_ref, z_ref, ab_ref, cw_ref, al_ref, dtb_ref, nw_ref, o_ref,
                xe_ref, qn_ref, kn_ref, vc_ref, gc_ref, beta_ref, s_ref,
                uv_ref, wk_ref, qd_ref, kd_ref, at_ref, gl_ref, *, rb):
    hd = GDN_HEAD_DIM
    nh = GDN_V_HEADS
    nqk = nh // 2
    qw = nqk * hd
    c = CHUNK
    halo = 8

    @pl.when(pl.program_id(0) == 0)
    def _():
        s_ref[...] = jnp.zeros_like(s_ref)
        xe_ref[0:halo, :] = jnp.zeros((halo, xe_ref.shape[1]), F32)

    xe_ref[halo:, 0:qw] = q_ref[...].astype(F32)
    xe_ref[halo:, qw:2 * qw] = k_ref[...].astype(F32)
    xe_ref[halo:, 2 * qw:] = v_ref[...].astype(F32)
    acc = cw_ref[GDN_CONV - 1:GDN_CONV, :] * xe_ref[halo:, :]
    for s in range(1, GDN_CONV):
        acc = acc + cw_ref[GDN_CONV - 1 - s:GDN_CONV - s, :] * xe_ref[halo - s:halo - s + rb, :]
    xe_ref[0:halo, :] = xe_ref[rb:rb + halo, :]
    qkv = _silu(acc)
    for p in range(nqk):
        qp = qkv[:, p * hd:(p + 1) * hd]
        qn_ref[:, p * hd:(p + 1) * hd] = qp * lax.rsqrt(jnp.sum(qp * qp, -1, keepdims=True) + EPS) * (hd ** -0.5)
        kp = qkv[:, qw + p * hd:qw + (p + 1) * hd]
        kn_ref[:, p * hd:(p + 1) * hd] = kp * lax.rsqrt(jnp.sum(kp * kp, -1, keepdims=True) + EPS)
    vc_ref[...] = qkv[:, 2 * qw:]

    ab = ab_ref[...]
    g = -jnp.exp(al_ref[...]) * jax.nn.softplus(ab + dtb_ref[...])
    beta_ref[...] = jax.nn.sigmoid(ab)
    row = lax.broadcasted_iota(jnp.int32, (rb, LANES), 0) % c
    for k in range(int(math.log2(c))):
        g = g + jnp.where(row >= (1 << k), pltpu.roll(g, 1 << k, 0), 0.0)
    gc_ref[...] = g

    ri = lax.broadcasted_iota(jnp.int32, (c, c), 0)
    ci = lax.broadcasted_iota(jnp.int32, (c, c), 1)
    causal = ri >= ci
    strict = ri > ci
    nw = nw_ref[...]

    def prepare_body(it, carry):
        rows = [pl.multiple_of((it * PREP + j) * c, c) for j in range(PREP)]
        scores = []
        for j, r0 in enumerate(rows):
            gcc = gc_ref[pl.ds(r0, c), :]
            gl_ref[it * PREP + j] = jnp.broadcast_to(jnp.exp(gcc[c - 1:c, :]), (8, LANES))
            for p in range(nqk):
                qp = qn_ref[pl.ds(r0, c), p * hd:(p + 1) * hd]
                kp = kn_ref[pl.ds(r0, c), p * hd:(p + 1) * hd]
                qk_kk = _dot_nt(jnp.concatenate([qp, kp], axis=0).astype(BF16), kp.astype(BF16))
                scores.append((r0, p, gcc, qp, kp, qk_kk))
        lows, rhss, where = [], [], []
        for r0, p, gcc, qp, kp, qk_kk in scores:
            gct = gcc.T
            bet = beta_ref[pl.ds(r0, c), :]
            for h in (2 * p, 2 * p + 1):
                cols = slice(h * hd, (h + 1) * hd)
                gcol = jnp.broadcast_to(gcc[:, h:h + 1], (c, hd))
                bcol = jnp.broadcast_to(bet[:, nh + h:nh + h + 1], (c, hd))
                dec = jnp.exp(jnp.where(causal, gcol[:, :c] - gct[h:h + 1, :], -jnp.inf))
                lows.append(jnp.where(strict, bcol[:, :c] * qk_kk[c:] * dec, 0.0))
                at_ref[pl.ds(r0, c), h * hd:h * hd + c] = (qk_kk[:c] * dec).astype(BF16)
                eg = jnp.exp(gcol)
                qd_ref[pl.ds(r0, c), cols] = (qp * eg).astype(BF16)
                kd_ref[pl.ds(r0, c), cols] = (kp * jnp.exp(gcc[c - 1:c, h:h + 1] - gcol)).astype(BF16)
                rhss.append(jnp.concatenate([vc_ref[pl.ds(r0, c), cols] * bcol, kp * (bcol * eg)],
                                            axis=1).astype(BF16))
                where.append((r0, cols))
        t_groups = _unit_lower_inverse([lows[i:i + 4] for i in range(0, len(lows), 4)])
        t_mats = [t for group in t_groups for t in group]
        uws = [_dot(t.astype(BF16), rhs) for t, rhs in zip(t_mats, rhss)]
        for uw, (r0, cols) in zip(uws, where):
            uv_ref[pl.ds(r0, c), cols] = uw[:, :hd]
            wk_ref[pl.ds(r0, c), cols] = uw[:, hd:].astype(BF16)
        return carry

    lax.fori_loop(0, rb // (PREP * c), prepare_body, 0)

    def state_body(n, carry):
        r0 = pl.multiple_of(n * c, c)
        gl = gl_ref[n]
        heads = [slice(h * hd, (h + 1) * hd) for h in range(nh)]
        sts = [s_ref[h] for h in range(nh)]
        ws_qs = [_dot(jnp.concatenate([wk_ref[pl.ds(r0, c), cols], qd_ref[pl.ds(r0, c), cols]], axis=0),
                      st.astype(BF16)) for cols, st in zip(heads, sts)]
        v16s = [(uv_ref[pl.ds(r0, c), cols] - wq[:c]).astype(BF16) for cols, wq in zip(heads, ws_qs)]
        upd = [_dot_tn(kd_ref[pl.ds(r0, c), cols], v16) for cols, v16 in zip(heads, v16s)]
        intra = [_dot(at_ref[pl.ds(r0, c), h * hd:h * hd + c], v16) for h, v16 in enumerate(v16s)]
        for h, cols in enumerate(heads):
            s_ref[h] = sts[h] * gl[0:1, h:h + 1] + upd[h]
            o = ws_qs[h][c:] + intra[h]
            zh = z_ref[pl.ds(r0, c), cols].astype(F32)
            on = o * lax.rsqrt(jnp.mean(o * o, -1, keepdims=True) + EPS) * nw
            o_ref[pl.ds(r0, c), cols] = (on * _silu(zh)).astype(o_ref.dtype)
        return carry

    lax.fori_loop(0, rb // c, state_body, 0)


def _gdn_mixer(proj, ab, conv_w, a_log, dt_bias, norm_w, l, rb):
    hd, nh = GDN_HEAD_DIM, GDN_V_HEADS
    qw = (nh // 2) * hd
    vw = nh * hd
    pad = LANES - nh
    al = jnp.pad(a_log.astype(F32), (0, pad)).reshape(1, LANES)
    dtb = jnp.pad(dt_bias.astype(F32), (0, pad)).reshape(1, LANES)
    kern = functools.partial(_gdn_kernel, rb=rb)
    return pl.pallas_call(
        kern,
        grid=(l // rb,),
        in_specs=[pl.BlockSpec((rb, qw), lambda i: (i, 2)),
                  pl.BlockSpec((rb, qw), lambda i: (i, 3)),
                  pl.BlockSpec((rb, vw), lambda i: (i, 2)),
                  pl.BlockSpec((rb, vw), lambda i: (i, 3)),
                  pl.BlockSpec((rb, LANES), lambda i: (i, 0)),
                  pl.BlockSpec((GDN_CONV, 2 * qw + vw), lambda i: (0, 0)),
                  pl.BlockSpec((1, LANES), lambda i: (0, 0)),
                  pl.BlockSpec((1, LANES), lambda i: (0, 0)),
                  pl.BlockSpec((1, hd), lambda i: (0, 0))],
        out_specs=pl.BlockSpec((rb, vw), lambda i: (i, 0)),
        out_shape=jax.ShapeDtypeStruct((l, vw), BF16),
        scratch_shapes=[pltpu.VMEM((rb + 8, 2 * qw + vw), F32),
                        pltpu.VMEM((rb, qw), F32),
                        pltpu.VMEM((rb, qw), F32),
                        pltpu.VMEM((rb, vw), F32),
                        pltpu.VMEM((rb, LANES), F32),
                        pltpu.VMEM((rb, LANES), F32),
                        pltpu.VMEM((nh, hd, hd), F32),
                        pltpu.VMEM((rb, vw), F32),
                        pltpu.VMEM((rb, vw), BF16),
                        pltpu.VMEM((rb, vw), BF16),
                        pltpu.VMEM((rb, vw), BF16),
                        pltpu.VMEM((rb, vw), BF16),
                        pltpu.VMEM((rb // CHUNK, 8, LANES), F32)],
        compiler_params=_params(("arbitrary",)),
        name="gdn_mixer",
    )(proj, proj, proj, proj, ab, conv_w.astype(F32), al, dtb, norm_w.astype(F32).reshape(1, hd))


def _gla_kernel(q_ref, k_ref, v_ref, r_ref, gl_ref, w2_ref, gb_ref, nw_ref, o_ref,
                qt_ref, kt_ref, gle_ref, st_ref, *, rb):
    nh = GLA_HEADS
    dk = q_ref.shape[1] // nh
    dv = v_ref.shape[1] // nh
    c = CHUNK

    @pl.when(pl.program_id(0) == 0)
    def _():
        st_ref[...] = jnp.zeros_like(st_ref)

    x = _dot_split(gl_ref[...], w2_ref[...]) + gb_ref[...]
    b = jax.nn.log_sigmoid(x) / GLA_TAU
    row = lax.broadcasted_iota(jnp.int32, b.shape, 0) % c
    for k in range(int(math.log2(c))):
        b = b + jnp.where(row >= (1 << k), pltpu.roll(b, 1 << k, 0), 0.0)
    q = q_ref[...].astype(F32) * (dk ** -0.5)
    kf = k_ref[...].astype(F32)
    qt_ref[...] = (q * jnp.exp(b)).astype(BF16)
    kt_ref[...] = (kf * jnp.exp(-b)).astype(BF16)
    for n in range(rb // c):
        gle_ref[n] = jnp.broadcast_to(jnp.exp(b[(n + 1) * c - 1:(n + 1) * c, :]), (8, b.shape[1]))

    ri = lax.broadcasted_iota(jnp.int32, (c, c), 0)
    ci = lax.broadcasted_iota(jnp.int32, (c, c), 1)
    causal = ri >= ci
    nw = nw_ref[...]

    def chunk_body(n, carry):
        r0 = pl.multiple_of(n * c, c)
        gle = gle_ref[n]
        qts = [qt_ref[pl.ds(r0, c), h * dk:(h + 1) * dk] for h in range(nh)]
        vhs = [v_ref[pl.ds(r0, c), h * dv:(h + 1) * dv] for h in range(nh)]
        sts = [st_ref[h] for h in range(nh)]
        kts = [kt_ref[pl.ds(r0, c), h * dk:(h + 1) * dk] for h in range(nh)]
        scores = [_dot_nt(qts[h], kts[h]) for h in range(nh)]
        inter = [_dot_nt(qts[h], sts[h].astype(BF16)) for h in range(nh)]
        upd = [_dot_tn(vhs[h], kts[h]) for h in range(nh)]
        intra = [_dot(jnp.where(causal, scores[h], 0.0).astype(BF16), vhs[h]) for h in range(nh)]
        for h in range(nh):
            st_ref[h] = (sts[h] + upd[h]) * gle[0:1, h * dk:(h + 1) * dk]
            o = intra[h] + inter[h]
            rh = r_ref[pl.ds(r0, c), h * dv:(h + 1) * dv].astype(F32)
            on = o * lax.rsqrt(jnp.mean(o * o, -1, keepdims=True) + EPS) * nw
            o_ref[pl.ds(r0, c), h * dv:(h + 1) * dv] = (on * _silu(rh)).astype(o_ref.dtype)
        return carry

    lax.fori_loop(0, rb // c, chunk_body, 0)


def _gla_mixer(proj, g_low, gate_w2, gate_b, norm_w, l, rb, dk_all, dv_all):
    nh = GLA_HEADS
    lowrank = gate_w2.shape[0]
    w2 = jnp.pad(gate_w2.astype(F32), ((0, LANES - lowrank), (0, 0)))
    kern = functools.partial(_gla_kernel, rb=rb)
    return pl.pallas_call(
        kern,
        grid=(l // rb,),
        in_specs=[pl.BlockSpec((rb, dk_all), lambda i: (i, 0)),
                  pl.BlockSpec((rb, dk_all), lambda i: (i, 1)),
                  pl.BlockSpec((rb, dv_all), lambda i: (i, 1)),
                  pl.BlockSpec((rb, dv_all), lambda i: (i, 2)),
                  pl.BlockSpec((rb, LANES), lambda i: (i, 0)),
                  pl.BlockSpec((LANES, dk_all), lambda i: (0, 0)),
                  pl.BlockSpec((1, dk_all), lambda i: (0, 0)),
                  pl.BlockSpec((1, dv_all // nh), lambda i: (0, 0))],
        out_specs=pl.BlockSpec((rb, dv_all), lambda i: (i, 0)),
        out_shape=jax.ShapeDtypeStruct((l, dv_all), BF16),
        scratch_shapes=[pltpu.VMEM((rb, dk_all), BF16),
                        pltpu.VMEM((rb, dk_all), BF16),
                        pltpu.VMEM((rb // CHUNK, 8, dk_all), F32),
                        pltpu.VMEM((nh, dv_all // nh, dk_all // nh), F32)],
        compiler_params=_params(("arbitrary",)),
        name="gla_mixer",
    )(proj, proj, proj, proj, g_low, w2, gate_b.astype(F32).reshape(1, dk_all),
      norm_w.astype(F32).reshape(1, dv_all // nh))


def _out0_kernel(x_ref, ya_ref, yb_ref, gw_ref, gb_ref, wa_ref, wb_ref, pw_ref, gt_ref, o_ref):
    ya = ya_ref[...]
    gate = jax.nn.sigmoid(_dot(ya, gw_ref[...]) + gb_ref[...])
    ya = (ya.astype(F32) * gate).astype(BF16)
    m = _dot(ya, wa_ref[...]) + _dot(yb_ref[...], wb_ref[...])
    o_ref[...] = _postnorm_residual(x_ref[...], m, pw_ref[...], gt_ref[...])


def _out1_kernel(x_ref, y_ref, w_ref, pw_ref, gt_ref, o_ref):
    m = _dot(y_ref[...], w_ref[...])
    o_ref[...] = _postnorm_residual(x_ref[...], m, pw_ref[...], gt_ref[...])


def _const_spec(shape):
    return pl.BlockSpec(shape, lambda i: (0,) * len(shape), pipeline_mode=pl.Buffered(1))


def _out_project0(x, ya, yb, glu_w, glu_b, w_out, post_w, mod, tm):
    l, d = x.shape
    wa = ya.shape[1]
    wb = yb.shape[1]
    return pl.pallas_call(
        _out0_kernel,
        grid=(l // tm,),
        in_specs=[pl.BlockSpec((tm, d), lambda i: (i, 0)),
                  pl.BlockSpec((tm, wa), lambda i: (i, 0)),
                  pl.BlockSpec((tm, wb), lambda i: (i, 0)),
                  _const_spec((wa, wa)),
                  _const_spec((1, wa)),
                  pl.BlockSpec((wa, d), lambda i: (0, 0), pipeline_mode=pl.Buffered(1)),
                  pl.BlockSpec((wb, d), lambda i: (1, 0), pipeline_mode=pl.Buffered(1)),
                  _const_spec((1, d)),
                  pl.BlockSpec((1, d), lambda i: (0, 2))],
        out_specs=pl.BlockSpec((tm, d), lambda i: (i, 0)),
        out_shape=jax.ShapeDtypeStruct((l, d), F32),
        compiler_params=_params(("arbitrary",)),
        name="out_project0",
    )(x, ya, yb, glu_w, glu_b.astype(F32).reshape(1, wa), w_out, w_out, post_w.reshape(1, d), mod)


def _out_project1(x, y, w_out, post_w, mod, tm):
    l, d = x.shape
    w = y.shape[1]
    return pl.pallas_call(
        _out1_kernel,
        grid=(l // tm,),
        in_specs=[pl.BlockSpec((tm, d), lambda i: (i, 0)),
                  pl.BlockSpec((tm, w), lambda i: (i, 0)),
                  _const_spec((w, d)),
                  _const_spec((1, d)),
                  pl.BlockSpec((1, d), lambda i: (0, 2))],
        out_specs=pl.BlockSpec((tm, d), lambda i: (i, 0)),
        out_shape=jax.ShapeDtypeStruct((l, d), F32),
        compiler_params=_params(("arbitrary",)),
        name="out_project1",
    )(x, y, w_out, post_w.reshape(1, d), mod)


def _ffn_kernel(x_ref, nw_ref, sc_ref, sh_ref, wg_ref, wu_ref, wd_ref, pw_ref, gt_ref, o_ref,
                h_ref, a_ref, *, n_hid, n_out):
    j = pl.program_id(1)
    th = a_ref.shape[2]
    tn = wd_ref.shape[1]

    @pl.when(j == 0)
    def _():
        h_ref[...] = _prenorm(x_ref[...], nw_ref[...], sc_ref[...], sh_ref[...]).astype(BF16)

    @pl.when(j < n_hid)
    def _():
        h = h_ref[...]
        a_ref[j] = (_silu(_dot(h, wg_ref[...])) * _dot(h, wu_ref[...])).astype(BF16)

    for jj in range(n_out):
        @pl.when(j == n_hid + jj)
        def _():
            acc = _dot(a_ref[0], wd_ref[0:th, :])
            for k in range(1, n_hid):
                acc = acc + _dot(a_ref[k], wd_ref[k * th:(k + 1) * th, :])
            o_ref[:, jj * tn:(jj + 1) * tn] = acc

    @pl.when(j == n_hid + n_out - 1)
    def _():
        o_ref[...] = _postnorm_residual(x_ref[...], o_ref[...], pw_ref[...], gt_ref[...])


def _ffn_block(x, mod, pre_w, post_w, w_gate, w_up, w_down, tm, th, tn):
    l, d = x.shape
    hid = w_gate.shape[1]
    n_hid, n_out = hid // th, d // tn
    kern = functools.partial(_ffn_kernel, n_hid=n_hid, n_out=n_out)
    return pl.pallas_call(
        kern,
        grid=(l // tm, n_hid + n_out),
        in_specs=[pl.BlockSpec((tm, d), lambda i, j: (i, 0)),
                  pl.BlockSpec((1, d), lambda i, j: (0, 0)),
                  pl.BlockSpec((1, d), lambda i, j: (0, 4)),
                  pl.BlockSpec((1, d), lambda i, j: (0, 3)),
                  pl.BlockSpec((d, th), lambda i, j: (0, jnp.minimum(j, n_hid - 1))),
                  pl.BlockSpec((d, th), lambda i, j: (0, jnp.minimum(j, n_hid - 1))),
                  pl.BlockSpec((hid, tn), lambda i, j: (0, jnp.maximum(j - n_hid, 0))),
                  pl.BlockSpec((1, d), lambda i, j: (0, 0)),
                  pl.BlockSpec((1, d), lambda i, j: (0, 5))],
        out_specs=pl.BlockSpec((tm, d), lambda i, j: (i, 0)),
        out_shape=jax.ShapeDtypeStruct((l, d), F32),
        scratch_shapes=[pltpu.VMEM((tm, d), BF16),
                        pltpu.VMEM((n_hid, tm, th), BF16)],
        compiler_params=_params(("arbitrary", "arbitrary")),
        name="ffn_block",
    )(x, pre_w.reshape(1, d), mod, mod, w_gate, w_up, w_down, post_w.reshape(1, d), mod)


def kernel(x, c, ada_w0, ada_b0, mix_pre0, mix_post0, ffn_pre0, ffn_post0, w_in0, s5_lambda_re, s5_lambda_im, s5_log_step, s5_b_re, s5_b_im, s5_c_re, s5_c_im, s5_d, s5_glu_w, s5_glu_b, gdn_conv_w, gdn_a_log, gdn_dt_bias, gdn_norm_w, w_out0, ffn_gate0, ffn_up0, ffn_down0, ada_w1, ada_b1, mix_pre1, mix_post1, ffn_pre1, ffn_post1, w_in1, gla_gate_w2, gla_gate_b, gla_norm_w, w_out1, ffn_gate1, ffn_up1, ffn_down1):
    bsz, l, d = x.shape
    assert bsz == 1
    x = x.reshape(l, d)
    tm = min(512, l)
    tp = min(1024, l)
    tf = min(512, l)
    th = 512
    tn = 512
    rb = min(512, l)
    cb = min(512, l // S5_T)
    n_hs = int(math.log2(cb))
    assert l % tm == 0 and l % rb == 0 and l % (cb * S5_T) == 0 and (1 << n_hs) == cb

    s5_w = s5_glu_w.shape[0]
    qk_w = (GDN_V_HEADS // 2) * GDN_HEAD_DIM
    v_w = GDN_V_HEADS * GDN_HEAD_DIM
    n0 = s5_w + 2 * qk_w + 2 * v_w
    dk_all = gla_gate_w2.shape[1]
    dv_all = w_out1.shape[0]
    n1 = 2 * dk_all + 2 * dv_all

    mod0 = _ada_modulation(c, ada_w0, ada_b0)
    mod1 = _ada_modulation(c, ada_w1, ada_b1)
    proj0, ab = _prenorm_project(x, mod0, 0, mix_pre0, w_in0, n0, tp, 1024)
    y_a = _s5_mixer(proj0, s5_lambda_re, s5_lambda_im, s5_log_step, s5_b_re, s5_b_im, s5_c_re, s5_c_im, s5_d,
                    l, cb, n_hs)
    y_b = _gdn_mixer(proj0, ab, gdn_conv_w, gdn_a_log, gdn_dt_bias, gdn_norm_w, l, rb)
    x = _out_project0(x, y_a, y_b, s5_glu_w.astype(BF16), s5_glu_b, w_out0.astype(BF16), mix_post0, mod0, tm)
    x = _ffn_block(x, mod0, ffn_pre0, ffn_post0, ffn_gate0.astype(BF16), ffn_up0.astype(BF16),
                   ffn_down0.astype(BF16), tf, th, tn)

    proj1, g_low = _prenorm_project(x, mod1, 0, mix_pre1, w_in1, n1, tp, 1024)
    y_c = _gla_mixer(proj1, g_low, gla_gate_w2, gla_gate_b, gla_norm_w, l, rb, dk_all, dv_all)
    x = _out_project1(x, y_c, w_out1.astype(BF16), mix_post1, mod1, tm)
    x = _ffn_block(x, mod1, ffn_pre1, ffn_post1, ffn_gate1.astype(BF16), ffn_up1.astype(BF16),
                   ffn_down1.astype(BF16), tf, th, tn)
    return x.reshape(bsz, l, d)
```

```python
import functools
import math

import jax
import jax.numpy as jnp
from jax import lax
from jax.experimental import pallas as pl
from jax.experimental.pallas import tpu as pltpu

F32 = jnp.float32
BF16 = jnp.bfloat16
EPS = 1e-6

LANES = 128
MXU_N = 256
ROW_SLAB = 64
VMEM_LIMIT = 56 << 20

S5_GROUP = 16
S5_STATE = 64
S5_T = 16
GDN_HEAD_DIM = 128
GDN_V_HEADS = 8
GDN_CONV = 4
CHUNK = 64
INV_BASE = 16
PREP = 8
GLA_HEADS = 4
GLA_TAU = 16.0
GLA_GROUP = 2


def _params(sem):
    return pltpu.CompilerParams(dimension_semantics=sem, vmem_limit_bytes=VMEM_LIMIT)


def _dot(a, b):
    return jnp.dot(a, b, preferred_element_type=F32)


def _dot_nt(a, b):
    return lax.dot_general(a, b, (((1,), (1,)), ((), ())), preferred_element_type=F32)


def _dot_tn(a, b):
    return lax.dot_general(a, b, (((0,), (0,)), ((), ())), preferred_element_type=F32)


def _split_bf16(x):
    hi = x.astype(BF16)
    lo = (x - hi.astype(F32)).astype(BF16)
    return hi, lo


def _dot_split(a, b):
    ah, al = _split_bf16(a)
    bh, bl = _split_bf16(b)
    return _dot(ah, bh) + (_dot(ah, bl) + _dot(al, bh))


def _silu(x):
    return x * jax.nn.sigmoid(x)


def _prenorm(x, nw, sc, sh):
    ms = jnp.mean(x * x, axis=-1, keepdims=True)
    return (x * lax.rsqrt(ms + EPS)) * (nw * (1.0 + sc)) + sh


def _postnorm_residual(x, m, pw, gt):
    ms = jnp.mean(m * m, axis=-1, keepdims=True)
    return x + (m * lax.rsqrt(ms + EPS)) * (gt * pw)


def _row_slabs(n_rows):
    slab = min(ROW_SLAB, n_rows)
    return [slice(s, s + slab) for s in range(0, n_rows, slab)]


def _shift_rows(x, s, row):
    return jnp.where(row >= s, pltpu.roll(x, s, 0), 0.0)


def _ada_kernel(c_ref, w_ref, b_ref, o_ref):
    s = _silu(c_ref[...])
    o_ref[...] = jnp.sum(s * w_ref[...], axis=0, keepdims=True) + b_ref[...]


def _ada_modulation(c, w, b):
    d, n = w.shape
    tn = 1024
    return pl.pallas_call(
        _ada_kernel,
        grid=(n // tn,),
        in_specs=[pl.BlockSpec((d, 1), lambda j: (0, 0)),
                  pl.BlockSpec((d, tn), lambda j: (0, j)),
                  pl.BlockSpec((1, tn), lambda j: (0, j))],
        out_specs=pl.BlockSpec((1, tn), lambda j: (0, j)),
        out_shape=jax.ShapeDtypeStruct((1, n), F32),
        compiler_params=_params(("arbitrary",)),
        name="ada_modulation",
    )(c.reshape(d, 1), w, b.reshape(1, n))


def _proj_kernel(x_ref, nw_ref, sc_ref, sh_ref, w_ref, ws_ref, o_ref, os_ref, h_ref):
    @pl.when(pl.program_id(1) == 0)
    def _():
        hb = _prenorm(x_ref[...], nw_ref[...], sc_ref[...], sh_ref[...]).astype(BF16)
        h_ref[...] = hb
        os_ref[...] = _dot(hb, ws_ref[...])

    o_ref[...] = _dot(h_ref[...], w_ref[...]).astype(o_ref.dtype)


def _prenorm_project(x, mod, mod_base, norm_w, w_all, n, tm, tn):
    l, d = x.shape
    w_main = w_all.astype(BF16)
    w_small = jnp.pad(w_all[:, n:], ((0, 0), (0, LANES - (w_all.shape[1] - n)))).astype(BF16)
    ns = LANES
    return pl.pallas_call(
        _proj_kernel,
        grid=(l // tm, n // tn),
        in_specs=[pl.BlockSpec((tm, d), lambda i, j: (i, 0)),
                  pl.BlockSpec((1, d), lambda i, j: (0, 0)),
                  pl.BlockSpec((1, d), lambda i, j: (0, mod_base + 1)),
                  pl.BlockSpec((1, d), lambda i, j: (0, mod_base)),
                  pl.BlockSpec((d, tn), lambda i, j: (0, j)),
                  pl.BlockSpec((d, ns), lambda i, j: (0, 0))],
        out_specs=[pl.BlockSpec((tm, tn), lambda i, j: (i, j)),
                   pl.BlockSpec((tm, ns), lambda i, j: (i, 0))],
        out_shape=[jax.ShapeDtypeStruct((l, n), BF16), jax.ShapeDtypeStruct((l, ns), F32)],
        scratch_shapes=[pltpu.VMEM((tm, d), BF16)],
        compiler_params=_params(("arbitrary", "arbitrary")),
        name="prenorm_project",
    )(x, norm_w.reshape(1, d), mod, mod, w_main, w_small)


def _s5_build_tables(b_ref, c_ref, lam_ref, d_ref, wi_ref, wb_ref, wc_ref, hs_ref, n_hs):
    t = S5_T
    half = b_ref.shape[2] // 2
    lr, li = lam_ref[0, 0:1, :], lam_ref[0, 1:2, :]
    dt = jnp.exp(lam_ref[0, 2:3, :])
    mag = jnp.exp(lr * dt)
    ar, ai = mag * jnp.cos(li * dt), mag * jnp.sin(li * dt)
    den = lr * lr + li * li
    nr, ni = ar - 1.0, ai
    f_re = (nr * lr + ni * li) / den
    f_im = (ni * lr - nr * li) / den
    b_re, b_im = b_ref[0, :, :half], b_ref[0, :, half:]
    bb_re = f_re * b_re - f_im * b_im
    bb_im = f_re * b_im + f_im * b_re
    c_re, c_im = c_ref[0, :, :half], c_ref[0, :, half:]
    bb_hi, bb_lo = _split_bf16(jnp.concatenate([bb_re, bb_im], axis=1))
    rr = lax.broadcasted_iota(jnp.int32, (LANES, LANES), 0)
    qq = lax.broadcasted_iota(jnp.int32, (LANES, LANES), 1)
    skip = jnp.where(rr == qq, d_ref[0], 0.0)

    pr, pi = jnp.ones_like(ar), jnp.zeros_like(ar)
    pows = []
    for d in range(t + 1):
        pows.append((pr, pi))
        cs = jnp.concatenate([c_re * pr - c_im * pi, -(c_re * pi + c_im * pr)], axis=1)
        if d < t:
            cs_hi, cs_lo = _split_bf16(cs)
            k_d = _dot_nt(bb_hi, cs_hi) + (_dot_nt(bb_hi, cs_lo) + _dot_nt(bb_lo, cs_hi))
            if d == 0:
                k_d = k_d + skip
            k_d = k_d.astype(BF16)
            for s in range(t - d):
                wi_ref[s * LANES:(s + 1) * LANES, (s + d) * LANES:(s + d + 1) * LANES] = k_d
        if d >= 1:
            wc_ref[:, (d - 1) * LANES:d * LANES] = cs.T.astype(BF16)
        pr, pi = pr * ar - pi * ai, pr * ai + pi * ar
    for tb in range(t * LANES // MXU_N):
        wi_ref[(2 * tb + 1) * LANES:(2 * tb + 2) * LANES, 2 * tb * LANES:(2 * tb + 1) * LANES] = (
            jnp.zeros((LANES, LANES), BF16))
    for s in range(t):
        pr, pi = pows[t - 1 - s]
        wb_ref[s * LANES:(s + 1) * LANES, :half] = (bb_re * pr - bb_im * pi).astype(BF16)
        wb_ref[s * LANES:(s + 1) * LANES, half:] = (bb_re * pi + bb_im * pr).astype(BF16)
    pr, pi = pows[t]
    for k in range(n_hs):
        hs_ref[k:k + 1, :half] = pr
        hs_ref[k:k + 1, half:] = pi
        pr, pi = pr * pr - pi * pi, 2.0 * pr * pi


def _s5_kernel(u_ref, b_ref, c_ref, lam_ref, d_ref, o_ref,
               wi_ref, wb_ref, wc_ref, hs_ref, uf_ref, u2_ref, yf_ref, carry_ref, *, cb, n_hs):
    t = S5_T
    half = carry_ref.shape[1] // 2

    @pl.when(pl.program_id(1) == 0)
    def _():
        carry_ref[...] = jnp.zeros_like(carry_ref)
        _s5_build_tables(b_ref, c_ref, lam_ref, d_ref, wi_ref, wb_ref, wc_ref, hs_ref, n_hs)

    uf_ref[...] = u_ref[...].astype(F32)
    for k in range(t):
        u2_ref[:, k * LANES:(k + 1) * LANES] = uf_ref[pl.ds(k, cb, stride=t), :].astype(BF16)
    u2 = u2_ref[...]

    s = _dot(u2, wb_ref[...])
    sr, si = s[:, :half], s[:, half:]
    row = lax.broadcasted_iota(jnp.int32, (cb, half), 0)
    cr, ci = carry_ref[:, :half], carry_ref[:, half:]
    pr, pi = hs_ref[0:1, :half], hs_ref[0:1, half:]
    sr = sr + jnp.where(row == 0, pr * cr - pi * ci, 0.0)
    si = si + jnp.where(row == 0, pr * ci + pi * cr, 0.0)
    for k in range(n_hs):
        pr, pi = hs_ref[k:k + 1, :half], hs_ref[k:k + 1, half:]
        shr, shi = _shift_rows(sr, 1 << k, row), _shift_rows(si, 1 << k, row)
        sr, si = sr + pr * shr - pi * shi, si + pr * shi + pi * shr
    epr = jnp.where(row >= 1, pltpu.roll(sr, 1, 0), cr)
    epi = jnp.where(row >= 1, pltpu.roll(si, 1, 0), ci)
    carry_ref[:, :half] = sr[cb - 1:cb]
    carry_ref[:, half:] = si[cb - 1:cb]
    ep = jnp.concatenate([epr, epi], axis=1).astype(BF16)
    y_inter = _dot(ep, wc_ref[...])

    for tb in range(t * LANES // MXU_N):
        kk = (tb + 1) * MXU_N
        yb = y_inter[:, tb * MXU_N:kk] + _dot(u2[:, :kk], wi_ref[:kk, tb * MXU_N:kk])
        yb = jax.nn.gelu(yb, approximate=True)
        for q in range(MXU_N // LANES):
            yf_ref[pl.ds(tb * (MXU_N // LANES) + q, cb, stride=t), :] = yb[:, q * LANES:(q + 1) * LANES]
    o_ref[...] = yf_ref[...].astype(o_ref.dtype)


def _s5_group_rows(x, nl):
    g, p, n = x.shape
    gl = g // nl
    x = jnp.tile(x.astype(F32).reshape(nl, gl * p, n), (1, 1, gl))
    rg = jnp.arange(gl * p)[:, None] // p
    cg = jnp.arange(gl * n)[None, :] // n
    return jnp.where(rg == cg, x, 0.0)


def _s5_mixer(proj, lam_re, lam_im, log_step, b_re, b_im, c_re, c_im, d_skip, l, cb, n_hs):
    t = S5_T
    g, n = lam_re.shape
    nl = g * S5_GROUP // LANES
    half = (g // nl) * n
    rb = cb * t
    brow = jnp.concatenate([_s5_group_rows(b_re.transpose(0, 2, 1), nl),
                            _s5_group_rows(b_im.transpose(0, 2, 1), nl)], axis=-1)
    crow = jnp.concatenate([_s5_group_rows(c_re, nl), _s5_group_rows(c_im, nl)], axis=-1)
    lam = jnp.stack([lam_re.astype(F32).reshape(nl, half), lam_im.astype(F32).reshape(nl, half),
                     jnp.repeat(log_step.astype(F32), n).reshape(nl, half)], axis=1)
    dsk = d_skip.astype(F32).reshape(nl, 1, LANES)
    kern = functools.partial(_s5_kernel, cb=cb, n_hs=n_hs)
    return pl.pallas_call(
        kern,
        grid=(nl, l // rb),
        in_specs=[pl.BlockSpec((rb, LANES), lambda a, b: (b, a)),
                  pl.BlockSpec((1, LANES, 2 * half), lambda a, b: (a, 0, 0)),
                  pl.BlockSpec((1, LANES, 2 * half), lambda a, b: (a, 0, 0)),
                  pl.BlockSpec((1, 3, half), lambda a, b: (a, 0, 0)),
                  pl.BlockSpec((1, 1, LANES), lambda a, b: (a, 0, 0))],
        out_specs=pl.BlockSpec((rb, LANES), lambda a, b: (b, a)),
        out_shape=jax.ShapeDtypeStruct((l, nl * LANES), BF16),
        scratch_shapes=[pltpu.VMEM((t * LANES, t * LANES), BF16),
                        pltpu.VMEM((t * LANES, 2 * half), BF16),
                        pltpu.VMEM((2 * half, t * LANES), BF16),
                        pltpu.VMEM((n_hs, 2 * half), F32),
                        pltpu.VMEM((rb, LANES), F32),
                        pltpu.VMEM((cb, t * LANES), BF16),
                        pltpu.VMEM((rb, LANES), F32),
                        pltpu.VMEM((1, 2 * half), F32)],
        compiler_params=_params(("arbitrary", "arbitrary")),
        name="s5_mixer",
    )(proj, brow, crow, lam, dsk)


def _unit_lower_inverse(lows):
    c = lows[0][0].shape[0]
    nb = len(lows[0])
    r = lax.broadcasted_iota(jnp.int32, (c, nb * c), 0)
    q = lax.broadcasted_iota(jnp.int32, (c, nb * c), 1)
    eye_w = jnp.where(q % c == r, 1.0, 0.0)
    rr = lax.broadcasted_iota(jnp.int32, (nb * c, nb * c), 0)
    qq = lax.broadcasted_iota(jnp.int32, (nb * c, nb * c), 1)
    on_diag = (rr // c) == (qq // c)

    def block_diag(xw):
        return jnp.where(on_diag, jnp.concatenate([xw] * nb, axis=0), 0.0).astype(BF16)

    def mm(xw, yw):
        return _dot(xw.astype(BF16), block_diag(yw))

    qc = q % c
    base = INV_BASE
    lws = [jnp.concatenate(group, axis=1) for group in lows]
    diag = [jnp.where(r // base == qc // base, lw, 0.0) for lw in lws]
    accs = [eye_w - d for d in diag]
    xs = [mm(d, d) for d in diag]
    steps = int(math.log2(base)) - 1
    for k in range(steps):
        last = k == steps - 1
        lhs = [acc if last else jnp.concatenate([x, acc], axis=0) for x, acc in zip(xs, accs)]
        both = [mm(a, x) for a, x in zip(lhs, xs)]
        accs = [acc + (b if last else b[c:]) for acc, b in zip(accs, both)]
        xs = [b[:c] for b in both]
    size = base
    while size < c:
        below = (r // (2 * size) == qc // (2 * size)) & (r // size != qc // size)
        subs = [jnp.where(below, lw, 0.0) for lw in lws]
        ys = [mm(s, acc) for s, acc in zip(subs, accs)]
        accs = [acc - mm(acc, y) for acc, y in zip(accs, ys)]
        size *= 2
    return [[acc[:, i * c:(i + 1) * c] for i in range(nb)] for acc in accs]


def _gdn_kernel(q_ref, k_ref, v_ref, z_ref, ab_ref, cw_ref, al_ref, dtb_ref, nw_ref, o_ref,
                xe_ref, qn_ref, kn_ref, vc_ref, gc_ref, beta_ref, s_ref,
                uv_ref, wk_ref, qd_ref, kd_ref, at_ref, gl_ref, *, rb):
    hd = GDN_HEAD_DIM
    nh = GDN_V_HEADS
    nqk = nh // 2
    qw = nqk * hd
    c = CHUNK
    halo = 8

    @pl.when(pl.program_id(0) == 0)
    def _():
        s_ref[...] = jnp.zeros_like(s_ref)
        xe_ref[0:halo, :] = jnp.zeros((halo, xe_ref.shape[1]), F32)

    xe_ref[halo:, 0:qw] = q_ref[...].astype(F32)
    xe_ref[halo:, qw:2 * qw] = k_ref[...].astype(F32)
    xe_ref[halo:, 2 * qw:] = v_ref[...].astype(F32)
    acc = cw_ref[GDN_CONV - 1:GDN_CONV, :] * xe_ref[halo:, :]
    for s in range(1, GDN_CONV):
        acc = acc + cw_ref[GDN_CONV - 1 - s:GDN_CONV - s, :] * xe_ref[halo - s:halo - s + rb, :]
    xe_ref[0:halo, :] = xe_ref[rb:rb + halo, :]
    qkv = _silu(acc)
    for p in range(nqk):
        qp = qkv[:, p * hd:(p + 1) * hd]
        qn_ref[:, p * hd:(p + 1) * hd] = qp * lax.rsqrt(jnp.sum(qp * qp, -1, keepdims=True) + EPS) * (hd ** -0.5)
        kp = qkv[:, qw + p * hd:qw + (p + 1) * hd]
        kn_ref[:, p * hd:(p + 1) * hd] = kp * lax.rsqrt(jnp.sum(kp * kp, -1, keepdims=True) + EPS)
    vc_ref[...] = qkv[:, 2 * qw:]

    ab = ab_ref[...]
    g = -jnp.exp(al_ref[...]) * jax.nn.softplus(ab + dtb_ref[...])
    beta_ref[...] = jax.nn.sigmoid(ab)
    row = lax.broadcasted_iota(jnp.int32, (rb, LANES), 0) % c
    for k in range(int(math.log2(c))):
        g = g + jnp.where(row >= (1 << k), pltpu.roll(g, 1 << k, 0), 0.0)
    gc_ref[...] = g

    ri = lax.broadcasted_iota(jnp.int32, (c, c), 0)
    ci = lax.broadcasted_iota(jnp.int32, (c, c), 1)
    causal = ri >= ci
    strict = ri > ci
    nw = nw_ref[...]

    def prepare_body(it, carry):
        rows = [pl.multiple_of((it * PREP + j) * c, c) for j in range(PREP)]
        scores = []
        for j, r0 in enumerate(rows):
            gcc = gc_ref[pl.ds(r0, c), :]
            gl_ref[it * PREP + j] = jnp.broadcast_to(jnp.exp(gcc[c - 1:c, :]), (8, LANES))
            for p in range(nqk):
                qp = qn_ref[pl.ds(r0, c), p * hd:(p + 1) * hd]
                kp = kn_ref[pl.ds(r0, c), p * hd:(p + 1) * hd]
                qk_kk = _dot_nt(jnp.concatenate([qp, kp], axis=0).astype(BF16), kp.astype(BF16))
                scores.append((r0, p, gcc, qp, kp, qk_kk))
        lows, rhss, where = [], [], []
        for r0, p, gcc, qp, kp, qk_kk in scores:
            gct = gcc.T
            bet = beta_ref[pl.ds(r0, c), :]
            for h in (2 * p, 2 * p + 1):
                cols = slice(h * hd, (h + 1) * hd)
                gcol = jnp.broadcast_to(gcc[:, h:h + 1], (c, hd))
                bcol = jnp.broadcast_to(bet[:, nh + h:nh + h + 1], (c, hd))
                dec = jnp.exp(jnp.where(causal, gcol[:, :c] - gct[h:h + 1, :], -jnp.inf))
                lows.append(jnp.where(strict, bcol[:, :c] * qk_kk[c:] * dec, 0.0))
                at_ref[pl.ds(r0, c), h * hd:h * hd + c] = (qk_kk[:c] * dec).astype(BF16)
                eg = jnp.exp(gcol)
                qd_ref[pl.ds(r0, c), cols] = (qp * eg).astype(BF16)
                kd_ref[pl.ds(r0, c), cols] = (kp * jnp.exp(gcc[c - 1:c, h:h + 1] - gcol)).astype(BF16)
                rhss.append(jnp.concatenate([vc_ref[pl.ds(r0, c), cols] * bcol, kp * (bcol * eg)],
                                            axis=1).astype(BF16))
                where.append((r0, cols))
        t_groups = _unit_lower_inverse([lows[i:i + 4] for i in range(0, len(lows), 4)])
        t_mats = [t for group in t_groups for t in group]
        uws = [_dot(t.astype(BF16), rhs) for t, rhs in zip(t_mats, rhss)]
        for uw, (r0, cols) in zip(uws, where):
            uv_ref[pl.ds(r0, c), cols] = uw[:, :hd]
            wk_ref[pl.ds(r0, c), cols] = uw[:, hd:].astype(BF16)
        return carry

    lax.fori_loop(0, rb // (PREP * c), prepare_body, 0)

    def state_body(n, carry):
        r0 = pl.multiple_of(n * c, c)
        gl = gl_ref[n]
        heads = [slice(h * hd, (h + 1) * hd) for h in range(nh)]
        sts = [s_ref[h] for h in range(nh)]
        ws_qs = [_dot(jnp.concatenate([wk_ref[pl.ds(r0, c), cols], qd_ref[pl.ds(r0, c), cols]], axis=0),
                      st.astype(BF16)) for cols, st in zip(heads, sts)]
        v16s = [(uv_ref[pl.ds(r0, c), cols] - wq[:c]).astype(BF16) for cols, wq in zip(heads, ws_qs)]
        upd = [_dot_tn(kd_ref[pl.ds(r0, c), cols], v16) for cols, v16 in zip(heads, v16s)]
        intra = [_dot(at_ref[pl.ds(r0, c), h * hd:h * hd + c], v16) for h, v16 in enumerate(v16s)]
        for h, cols in enumerate(heads):
            s_ref[h] = sts[h] * gl[0:1, h:h + 1] + upd[h]
            o = ws_qs[h][c:] + intra[h]
            zh = z_ref[pl.ds(r0, c), cols].astype(F32)
            on = o * lax.rsqrt(jnp.mean(o * o, -1, keepdims=True) + EPS) * nw
            o_ref[pl.ds(r0, c), cols] = (on * _silu(zh)).astype(o_ref.dtype)
        return carry

    lax.fori_loop(0, rb // c, state_body, 0)


def _gdn_mixer(proj, ab, conv_w, a_log, dt_bias, norm_w, l, rb):
    hd, nh = GDN_HEAD_DIM, GDN_V_HEADS
    qw = (nh // 2) * hd
    vw = nh * hd
    pad = LANES - nh
    al = jnp.pad(a_log.astype(F32), (0, pad)).reshape(1, LANES)
    dtb = jnp.pad(dt_bias.astype(F32), (0, pad)).reshape(1, LANES)
    kern = functools.partial(_gdn_kernel, rb=rb)
    return pl.pallas_call(
        kern,
        grid=(l // rb,),
        in_specs=[pl.BlockSpec((rb, qw), lambda i: (i, 2)),
                  pl.BlockSpec((rb, qw), lambda i: (i, 3)),
                  pl.BlockSpec((rb, vw), lambda i: (i, 2)),
                  pl.BlockSpec((rb, vw), lambda i: (i, 3)),
                  pl.BlockSpec((rb, LANES), lambda i: (i, 0)),
                  pl.BlockSpec((GDN_CONV, 2 * qw + vw), lambda i: (0, 0)),
                  pl.BlockSpec((1, LANES), lambda i: (0, 0)),
                  pl.BlockSpec((1, LANES), lambda i: (0, 0)),
                  pl.BlockSpec((1, hd), lambda i: (0, 0))],
        out_specs=pl.BlockSpec((rb, vw), lambda i: (i, 0)),
        out_shape=jax.ShapeDtypeStruct((l, vw), BF16),
        scratch_shapes=[pltpu.VMEM((rb + 8, 2 * qw + vw), F32),
                        pltpu.VMEM((rb, qw), F32),
                        pltpu.VMEM((rb, qw), F32),
                        pltpu.VMEM((rb, vw), F32),
                        pltpu.VMEM((rb, LANES), F32),
                        pltpu.VMEM((rb, LANES), F32),
                        pltpu.VMEM((nh, hd, hd), F32),
                        pltpu.VMEM((rb, vw), F32),
                        pltpu.VMEM((rb, vw), BF16),
                        pltpu.VMEM((rb, vw), BF16),
                        pltpu.VMEM((rb, vw), BF16),
                        pltpu.VMEM((rb, vw), BF16),
                        pltpu.VMEM((rb // CHUNK, 8, LANES), F32)],
        compiler_params=_params(("arbitrary",)),
        name="gdn_mixer",
    )(proj, proj, proj, proj, ab, conv_w.astype(F32), al, dtb, norm_w.astype(F32).reshape(1, hd))


def _gla_kernel(q_ref, k_ref, v_ref, r_ref, gl_ref, w2_ref, gb_ref, nw_ref, o_ref,
                qt_ref, kt_ref, gle_ref, st_ref, *, rb):
    nh = GLA_HEADS
    dk = q_ref.shape[1] // nh
    dv = v_ref.shape[1] // nh
    c = CHUNK

    @pl.when(pl.program_id(0) == 0)
    def _():
        st_ref[...] = jnp.zeros_like(st_ref)

    x = _dot_split(gl_ref[...], w2_ref[...]) + gb_ref[...]
    b = jax.nn.log_sigmoid(x) / GLA_TAU
    row = lax.broadcasted_iota(jnp.int32, b.shape, 0) % c
    for k in range(int(math.log2(c))):
        b = b + jnp.where(row >= (1 << k), pltpu.roll(b, 1 << k, 0), 0.0)
    q = q_ref[...].astype(F32) * (dk ** -0.5)
    kf = k_ref[...].astype(F32)
    qt_ref[...] = (q * jnp.exp(b)).astype(BF16)
    kt_ref[...] = (kf * jnp.exp(-b)).astype(BF16)
    for n in range(rb // c):
        gle_ref[n] = jnp.broadcast_to(jnp.exp(b[(n + 1) * c - 1:(n + 1) * c, :]), (8, b.shape[1]))

    ri = lax.broadcasted_iota(jnp.int32, (c, c), 0)
    ci = lax.broadcasted_iota(jnp.int32, (c, c), 1)
    causal = ri >= ci
    nw = nw_ref[...]

    def chunk_body(it, carry):
        sts = [st_ref[h] for h in range(nh)]
        work = []
        for j in range(GLA_GROUP):
            n = it * GLA_GROUP + j
            r0 = pl.multiple_of(n * c, c)
            qts = [qt_ref[pl.ds(r0, c), h * dk:(h + 1) * dk] for h in range(nh)]
            kts = [kt_ref[pl.ds(r0, c), h * dk:(h + 1) * dk] for h in range(nh)]
            vhs = [v_ref[pl.ds(r0, c), h * dv:(h + 1) * dv] for h in range(nh)]
            scores = [_dot_nt(qts[h], kts[h]) for h in range(nh)]
            upd = [_dot_tn(vhs[h], kts[h]) for h in range(nh)]
            work.append((n, r0, qts, vhs, scores, upd))
        for n, r0, qts, vhs, scores, upd in work:
            gle = gle_ref[n]
            inter = [_dot_nt(qts[h], sts[h].astype(BF16)) for h in range(nh)]
            intra = [_dot(jnp.where(causal, scores[h], 0.0).astype(BF16), vhs[h]) for h in range(nh)]
            sts = [(sts[h] + upd[h]) * gle[0:1, h * dk:(h + 1) * dk] for h in range(nh)]
            for h in range(nh):
                o = intra[h] + inter[h]
                rh = r_ref[pl.ds(r0, c), h * dv:(h + 1) * dv].astype(F32)
                on = o * lax.rsqrt(jnp.mean(o * o, -1, keepdims=True) + EPS) * nw
                o_ref[pl.ds(r0, c), h * dv:(h + 1) * dv] = (on * _silu(rh)).astype(o_ref.dtype)
        for h in range(nh):
            st_ref[h] = sts[h]
        return carry

    lax.fori_loop(0, rb // (GLA_GROUP * c), chunk_body, 0)


def _gla_mixer(proj, g_low, gate_w2, gate_b, norm_w, l, rb, dk_all, dv_all):
    nh = GLA_HEADS
    lowrank = gate_w2.shape[0]
    w2 = jnp.pad(gate_w2.astype(F32), ((0, LANES - lowrank), (0, 0)))
    kern = functools.partial(_gla_kernel, rb=rb)
    return pl.pallas_call(
        kern,
        grid=(l // rb,),
        in_specs=[pl.BlockSpec((rb, dk_all), lambda i: (i, 0)),
                  pl.BlockSpec((rb, dk_all), lambda i: (i, 1)),
                  pl.BlockSpec((rb, dv_all), lambda i: (i, 1)),
                  pl.BlockSpec((rb, dv_all), lambda i: (i, 2)),
                  pl.BlockSpec((rb, LANES), lambda i: (i, 0)),
                  pl.BlockSpec((LANES, dk_all), lambda i: (0, 0)),
                  pl.BlockSpec((1, dk_all), lambda i: (0, 0)),
                  pl.BlockSpec((1, dv_all // nh), lambda i: (0, 0))],
        out_specs=pl.BlockSpec((rb, dv_all), lambda i: (i, 0)),
        out_shape=jax.ShapeDtypeStruct((l, dv_all), BF16),
        scratch_shapes=[pltpu.VMEM((rb, dk_all), BF16),
                        pltpu.VMEM((rb, dk_all), BF16),
                        pltpu.VMEM((rb // CHUNK, 8, dk_all), F32),
                        pltpu.VMEM((nh, dv_all // nh, dk_all // nh), F32)],
        compiler_params=_params(("arbitrary",)),
        name="gla_mixer",
    )(proj, proj, proj, proj, g_low, w2, gate_b.astype(F32).reshape(1, dk_all),
      norm_w.astype(F32).reshape(1, dv_all // nh))


def _out0_kernel(x_ref, ya_ref, yb_ref, gw_ref, gb_ref, wa_ref, wb_ref, pw_ref, gt_ref, o_ref):
    ya = ya_ref[...]
    gate = jax.nn.sigmoid(_dot(ya, gw_ref[...]) + gb_ref[...])
    ya = (ya.astype(F32) * gate).astype(BF16)
    m = _dot(ya, wa_ref[...]) + _dot(yb_ref[...], wb_ref[...])
    o_ref[...] = _postnorm_residual(x_ref[...], m, pw_ref[...], gt_ref[...])


def _out1_kernel(x_ref, y_ref, w_ref, pw_ref, gt_ref, o_ref):
    m = _dot(y_ref[...], w_ref[...])
    o_ref[...] = _postnorm_residual(x_ref[...], m, pw_ref[...], gt_ref[...])


def _const_spec(shape):
    return pl.BlockSpec(shape, lambda i: (0,) * len(shape), pipeline_mode=pl.Buffered(1))


def _out_project0(x, ya, yb, glu_w, glu_b, w_out, post_w, mod, tm):
    l, d = x.shape
    wa = ya.shape[1]
    wb = yb.shape[1]
    return pl.pallas_call(
        _out0_kernel,
        grid=(l // tm,),
        in_specs=[pl.BlockSpec((tm, d), lambda i: (i, 0)),
                  pl.BlockSpec((tm, wa), lambda i: (i, 0)),
                  pl.BlockSpec((tm, wb), lambda i: (i, 0)),
                  _const_spec((wa, wa)),
                  _const_spec((1, wa)),
                  pl.BlockSpec((wa, d), lambda i: (0, 0), pipeline_mode=pl.Buffered(1)),
                  pl.BlockSpec((wb, d), lambda i: (1, 0), pipeline_mode=pl.Buffered(1)),
                  _const_spec((1, d)),
                  pl.BlockSpec((1, d), lambda i: (0, 2))],
        out_specs=pl.BlockSpec((tm, d), lambda i: (i, 0)),
        out_shape=jax.ShapeDtypeStruct((l, d), F32),
        compiler_params=_params(("arbitrary",)),
        name="out_project0",
    )(x, ya, yb, glu_w, glu_b.astype(F32).reshape(1, wa), w_out, w_out, post_w.reshape(1, d), mod)


def _out_project1(x, y, w_out, post_w, mod, tm):
    l, d = x.shape
    w = y.shape[1]
    return pl.pallas_call(
        _out1_kernel,
        grid=(l // tm,),
        in_specs=[pl.BlockSpec((tm, d), lambda i: (i, 0)),
                  pl.BlockSpec((tm, w), lambda i: (i, 0)),
                  _const_spec((w, d)),
                  _const_spec((1, d)),
                  pl.BlockSpec((1, d), lambda i: (0, 2))],
        out_specs=pl.BlockSpec((tm, d), lambda i: (i, 0)),
        out_shape=jax.ShapeDtypeStruct((l, d), F32),
        compiler_params=_params(("arbitrary",)),
        name="out_project1",
    )(x, y, w_out, post_w.reshape(1, d), mod)


def _ffn_kernel(x_ref, nw_ref, sc_ref, sh_ref, wg_ref, wu_ref, wd_ref, pw_ref, gt_ref, o_ref,
                h_ref, a_ref, *, n_hid, n_out):
    j = pl.program_id(1)
    th = a_ref.shape[2]
    tn = wd_ref.shape[1]

    @pl.when(j == 0)
    def _():
        for r in _row_slabs(x_ref.shape[0]):
            h_ref[r, :] = _prenorm(x_ref[r, :], nw_ref[...], sc_ref[...], sh_ref[...]).astype(BF16)

    @pl.when(j < n_hid)
    def _():
        h = h_ref[...]
        a_ref[j] = (_silu(_dot(h, wg_ref[...])) * _dot(h, wu_ref[...])).astype(BF16)

    for jj in range(n_out):
        @pl.when(j == n_hid + jj)
        def _():
            acc = _dot(a_ref[0], wd_ref[0:th, :])
            for k in range(1, n_hid):
                acc = acc + _dot(a_ref[k], wd_ref[k * th:(k + 1) * th, :])
            o_ref[:, jj * tn:(jj + 1) * tn] = acc

    @pl.when(j == n_hid + n_out - 1)
    def _():
        for r in _row_slabs(x_ref.shape[0]):
            o_ref[r, :] = _postnorm_residual(x_ref[r, :], o_ref[r, :], pw_ref[...], gt_ref[...])


def _ffn_block(x, mod, pre_w, post_w, w_gate, w_up, w_down, tm, th, tn):
    l, d = x.shape
    hid = w_gate.shape[1]
    n_hid, n_out = hid // th, d // tn
    kern = functools.partial(_ffn_kernel, n_hid=n_hid, n_out=n_out)
    return pl.pallas_call(
        kern,
        grid=(l // tm, n_hid + n_out),
        in_specs=[pl.BlockSpec((tm, d), lambda i, j: (i, 0)),
                  pl.BlockSpec((1, d), lambda i, j: (0, 0)),
                  pl.BlockSpec((1, d), lambda i, j: (0, 4)),
                  pl.BlockSpec((1, d), lambda i, j: (0, 3)),
                  pl.BlockSpec((d, th), lambda i, j: (0, jnp.minimum(j, n_hid - 1))),
                  pl.BlockSpec((d, th), lambda i, j: (0, jnp.minimum(j, n_hid - 1))),
                  pl.BlockSpec((hid, tn), lambda i, j: (0, jnp.maximum(j - n_hid, 0))),
                  pl.BlockSpec((1, d), lambda i, j: (0, 0)),
                  pl.BlockSpec((1, d), lambda i, j: (0, 5))],
        out_specs=pl.BlockSpec((tm, d), lambda i, j: (i, 0)),
        out_shape=jax.ShapeDtypeStruct((l, d), F32),
        scratch_shapes=[pltpu.VMEM((tm, d), BF16),
                        pltpu.VMEM((n_hid, tm, th), BF16)],
        compiler_params=_params(("arbitrary", "arbitrary")),
        name="ffn_block",
    )(x, pre_w.reshape(1, d), mod, mod, w_gate, w_up, w_down, post_w.reshape(1, d), mod)


def kernel(x, c, ada_w0, ada_b0, mix_pre0, mix_post0, ffn_pre0, ffn_post0, w_in0, s5_lambda_re, s5_lambda_im, s5_log_step, s5_b_re, s5_b_im, s5_c_re, s5_c_im, s5_d, s5_glu_w, s5_glu_b, gdn_conv_w, gdn_a_log, gdn_dt_bias, gdn_norm_w, w_out0, ffn_gate0, ffn_up0, ffn_down0, ada_w1, ada_b1, mix_pre1, mix_post1, ffn_pre1, ffn_post1, w_in1, gla_gate_w2, gla_gate_b, gla_norm_w, w_out1, ffn_gate1, ffn_up1, ffn_down1):
    bsz, l, d = x.shape
    assert bsz == 1
    x = x.reshape(l, d)
    tm = min(512, l)
    tp = min(1024, l)
    tf = min(512, l)
    th = 512
    tn = 512
    rb = min(512, l)
    cb = min(512, l // S5_T)
    n_hs = int(math.log2(cb))
    assert l % tm == 0 and l % rb == 0 and l % (cb * S5_T) == 0 and (1 << n_hs) == cb

    s5_w = s5_glu_w.shape[0]
    qk_w = (GDN_V_HEADS // 2) * GDN_HEAD_DIM
    v_w = GDN_V_HEADS * GDN_HEAD_DIM
    n0 = s5_w + 2 * qk_w + 2 * v_w
    dk_all = gla_gate_w2.shape[1]
    dv_all = w_out1.shape[0]
    n1 = 2 * dk_all + 2 * dv_all

    mod0 = _ada_modulation(c, ada_w0, ada_b0)
    mod1 = _ada_modulation(c, ada_w1, ada_b1)
    tc = 1024
    proj0, ab = _prenorm_project(x, mod0, 0, mix_pre0, w_in0, n0, tp, tc)
    y_a = _s5_mixer(proj0, s5_lambda_re, s5_lambda_im, s5_log_step, s5_b_re, s5_b_im, s5_c_re, s5_c_im, s5_d,
                    l, cb, n_hs)
    y_b = _gdn_mixer(proj0, ab, gdn_conv_w, gdn_a_log, gdn_dt_bias, gdn_norm_w, l, rb)
    x = _out_project0(x, y_a, y_b, s5_glu_w.astype(BF16), s5_glu_b, w_out0.astype(BF16), mix_post0, mod0, tm)
    x = _ffn_block(x, mod0, ffn_pre0, ffn_post0, ffn_gate0.astype(BF16), ffn_up0.astype(BF16),
                   ffn_down0.astype(BF16), tf, th, tn)

    proj1, g_low = _prenorm_project(x, mod1, 0, mix_pre1, w_in1, n1, tp, tc)
    y_c = _gla_mixer(proj1, g_low, gla_gate_w2, gla_gate_b, gla_norm_w, l, rb, dk_all, dv_all)
    x = _out_project1(x, y_c, w_out1.astype(BF16), mix_post1, mod1, tm)
    x = _ffn_block(x, mod1, ffn_pre1, ffn_post1, ffn_gate1.astype(BF16), ffn_up1.astype(BF16),
                   ffn_down1.astype(BF16), tf, th, tn)
    return x.reshape(bsz, l, d)
```

```python
import functools
import math

import jax
import jax.numpy as jnp
from jax import lax
from jax.experimental import pallas as pl
from jax.experimental.pallas import tpu as pltpu

F32 = jnp.float32
BF16 = jnp.bfloat16
EPS = 1e-6

LANES = 128
MXU_N = 256
ROW_SLAB = 64
VMEM_LIMIT = 56 << 20

S5_GROUP = 16
S5_STATE = 64
S5_T = 16
GDN_HEAD_DIM = 128
GDN_V_HEADS = 8
GDN_CONV = 4
CHUNK = 64
INV_BASE = 16
PREP = 8
GLA_HEADS = 4
GLA_TAU = 16.0
GLA_GROUP = 2


def _params(sem):
    return pltpu.CompilerParams(dimension_semantics=sem, vmem_limit_bytes=VMEM_LIMIT)


def _dot(a, b):
    return jnp.dot(a, b, preferred_element_type=F32)


def _dot_nt(a, b):
    return lax.dot_general(a, b, (((1,), (1,)), ((), ())), preferred_element_type=F32)


def _dot_tn(a, b):
    return lax.dot_general(a, b, (((0,), (0,)), ((), ())), preferred_element_type=F32)


def _split_bf16(x):
    hi = x.astype(BF16)
    lo = (x - hi.astype(F32)).astype(BF16)
    return hi, lo


def _dot_split(a, b):
    ah, al = _split_bf16(a)
    bh, bl = _split_bf16(b)
    return _dot(ah, bh) + (_dot(ah, bl) + _dot(al, bh))


def _silu(x):
    return x * jax.nn.sigmoid(x)


def _prenorm(x, nw, sc, sh):
    ms = jnp.mean(x * x, axis=-1, keepdims=True)
    return (x * lax.rsqrt(ms + EPS)) * (nw * (1.0 + sc)) + sh


def _postnorm_residual(x, m, pw, gt):
    ms = jnp.mean(m * m, axis=-1, keepdims=True)
    return x + (m * lax.rsqrt(ms + EPS)) * (gt * pw)


def _row_slabs(n_rows):
    slab = min(ROW_SLAB, n_rows)
    return [slice(s, s + slab) for s in range(0, n_rows, slab)]


def _shift_rows(x, s, row):
    return jnp.where(row >= s, pltpu.roll(x, s, 0), 0.0)


def _ada_kernel(c_ref, w_ref, b_ref, o_ref):
    s = _silu(c_ref[...])
    o_ref[...] = jnp.sum(s * w_ref[...], axis=0, keepdims=True) + b_ref[...]


def _ada_modulation(c, w, b):
    d, n = w.shape
    tn = 1024
    return pl.pallas_call(
        _ada_kernel,
        grid=(n // tn,),
        in_specs=[pl.BlockSpec((d, 1), lambda j: (0, 0)),
                  pl.BlockSpec((d, tn), lambda j: (0, j)),
                  pl.BlockSpec((1, tn), lambda j: (0, j))],
        out_specs=pl.BlockSpec((1, tn), lambda j: (0, j)),
        out_shape=jax.ShapeDtypeStruct((1, n), F32),
        compiler_params=_params(("arbitrary",)),
        name="ada_modulation",
    )(c.reshape(d, 1), w, b.reshape(1, n))


def _proj_kernel(x_ref, nw_ref, sc_ref, sh_ref, w_ref, ws_ref, o_ref, os_ref, h_ref):
    @pl.when(pl.program_id(1) == 0)
    def _():
        hb = _prenorm(x_ref[...], nw_ref[...], sc_ref[...], sh_ref[...]).astype(BF16)
        h_ref[...] = hb
        os_ref[...] = _dot(hb, ws_ref[...])

    o_ref[...] = _dot(h_ref[...], w_ref[...]).astype(o_ref.dtype)


def _prenorm_project(x, mod, mod_base, norm_w, w_all, n, tm, tn):
    l, d = x.shape
    w_main = w_all.astype(BF16)
    w_small = jnp.pad(w_all[:, n:], ((0, 0), (0, LANES - (w_all.shape[1] - n)))).astype(BF16)
    ns = LANES
    return pl.pallas_call(
        _proj_kernel,
        grid=(l // tm, n // tn),
        in_specs=[pl.BlockSpec((tm, d), lambda i, j: (i, 0)),
                  pl.BlockSpec((1, d), lambda i, j: (0, 0)),
                  pl.BlockSpec((1, d), lambda i, j: (0, mod_base + 1)),
                  pl.BlockSpec((1, d), lambda i, j: (0, mod_base)),
                  pl.BlockSpec((d, tn), lambda i, j: (0, j)),
                  pl.BlockSpec((d, ns), lambda i, j: (0, 0))],
        out_specs=[pl.BlockSpec((tm, tn), lambda i, j: (i, j)),
                   pl.BlockSpec((tm, ns), lambda i, j: (i, 0))],
        out_shape=[jax.ShapeDtypeStruct((l, n), BF16), jax.ShapeDtypeStruct((l, ns), F32)],
        scratch_shapes=[pltpu.VMEM((tm, d), BF16)],
        compiler_params=_params(("arbitrary", "arbitrary")),
        name="prenorm_project",
    )(x, norm_w.reshape(1, d), mod, mod, w_main, w_small)


def _s5_build_tables(b_ref, c_ref, lam_ref, d_ref, wi_ref, wb_ref, wc_ref, hs_ref, n_hs):
    t = S5_T
    half = b_ref.shape[2] // 2
    lr, li = lam_ref[0, 0:1, :], lam_ref[0, 1:2, :]
    dt = jnp.exp(lam_ref[0, 2:3, :])
    mag = jnp.exp(lr * dt)
    ar, ai = mag * jnp.cos(li * dt), mag * jnp.sin(li * dt)
    den = lr * lr + li * li
    nr, ni = ar - 1.0, ai
    f_re = (nr * lr + ni * li) / den
    f_im = (ni * lr - nr * li) / den
    b_re, b_im = b_ref[0, :, :half], b_ref[0, :, half:]
    bb_re = f_re * b_re - f_im * b_im
    bb_im = f_re * b_im + f_im * b_re
    c_re, c_im = c_ref[0, :, :half], c_ref[0, :, half:]
    bb_hi, bb_lo = _split_bf16(jnp.concatenate([bb_re, bb_im], axis=1))
    rr = lax.broadcasted_iota(jnp.int32, (LANES, LANES), 0)
    qq = lax.broadcasted_iota(jnp.int32, (LANES, LANES), 1)
    skip = jnp.where(rr == qq, d_ref[0], 0.0)

    pr, pi = jnp.ones_like(ar), jnp.zeros_like(ar)
    pows = []
    for d in range(t + 1):
        pows.append((pr, pi))
        cs = jnp.concatenate([c_re * pr - c_im * pi, -(c_re * pi + c_im * pr)], axis=1)
        if d < t:
            cs_hi, cs_lo = _split_bf16(cs)
            k_d = _dot_nt(bb_hi, cs_hi) + (_dot_nt(bb_hi, cs_lo) + _dot_nt(bb_lo, cs_hi))
            if d == 0:
                k_d = k_d + skip
            k_d = k_d.astype(BF16)
            for s in range(t - d):
                wi_ref[s * LANES:(s + 1) * LANES, (s + d) * LANES:(s + d + 1) * LANES] = k_d
        if d >= 1:
            wc_ref[:, (d - 1) * LANES:d * LANES] = cs.T.astype(BF16)
        pr, pi = pr * ar - pi * ai, pr * ai + pi * ar
    for tb in range(t * LANES // MXU_N):
        wi_ref[(2 * tb + 1) * LANES:(2 * tb + 2) * LANES, 2 * tb * LANES:(2 * tb + 1) * LANES] = (
            jnp.zeros((LANES, LANES), BF16))
    for s in range(t):
        pr, pi = pows[t - 1 - s]
        wb_ref[s * LANES:(s + 1) * LANES, :half] = (bb_re * pr - bb_im * pi).astype(BF16)
        wb_ref[s * LANES:(s + 1) * LANES, half:] = (bb_re * pi + bb_im * pr).astype(BF16)
    pr, pi = pows[t]
    for k in range(n_hs):
        hs_ref[k:k + 1, :half] = pr
        hs_ref[k:k + 1, half:] = pi
        pr, pi = pr * pr - pi * pi, 2.0 * pr * pi


def _s5_kernel(u_ref, b_ref, c_ref, lam_ref, d_ref, o_ref,
               wi_ref, wb_ref, wc_ref, hs_ref, uf_ref, u2_ref, yf_ref, carry_ref, *, cb, n_hs):
    t = S5_T
    half = carry_ref.shape[1] // 2

    @pl.when(pl.program_id(1) == 0)
    def _():
        carry_ref[...] = jnp.zeros_like(carry_ref)
        _s5_build_tables(b_ref, c_ref, lam_ref, d_ref, wi_ref, wb_ref, wc_ref, hs_ref, n_hs)

    uf_ref[...] = u_ref[...].astype(F32)
    for k in range(t):
        u2_ref[:, k * LANES:(k + 1) * LANES] = uf_ref[pl.ds(k, cb, stride=t), :].astype(BF16)
    u2 = u2_ref[...]

    s = _dot(u2, wb_ref[...])
    sr, si = s[:, :half], s[:, half:]
    row = lax.broadcasted_iota(jnp.int32, (cb, half), 0)
    cr, ci = carry_ref[:, :half], carry_ref[:, half:]
    pr, pi = hs_ref[0:1, :half], hs_ref[0:1, half:]
    sr = sr + jnp.where(row == 0, pr * cr - pi * ci, 0.0)
    si = si + jnp.where(row == 0, pr * ci + pi * cr, 0.0)
    for k in range(n_hs):
        pr, pi = hs_ref[k:k + 1, :half], hs_ref[k:k + 1, half:]
        shr, shi = _shift_rows(sr, 1 << k, row), _shift_rows(si, 1 << k, row)
        sr, si = sr + pr * shr - pi * shi, si + pr * shi + pi * shr
    epr = jnp.where(row >= 1, pltpu.roll(sr, 1, 0), cr)
    epi = jnp.where(row >= 1, pltpu.roll(si, 1, 0), ci)
    carry_ref[:, :half] = sr[cb - 1:cb]
    carry_ref[:, half:] = si[cb - 1:cb]
    ep = jnp.concatenate([epr, epi], axis=1).astype(BF16)
    y_inter = _dot(ep, wc_ref[...])

    for tb in range(t * LANES // MXU_N):
        kk = (tb + 1) * MXU_N
        yb = y_inter[:, tb * MXU_N:kk] + _dot(u2[:, :kk], wi_ref[:kk, tb * MXU_N:kk])
        yb = jax.nn.gelu(yb, approximate=True)
        for q in range(MXU_N // LANES):
            yf_ref[pl.ds(tb * (MXU_N // LANES) + q, cb, stride=t), :] = yb[:, q * LANES:(q + 1) * LANES]
    o_ref[...] = yf_ref[...].astype(o_ref.dtype)


def _s5_group_rows(x, nl):
    g, p, n = x.shape
    gl = g // nl
    x = jnp.tile(x.astype(F32).reshape(nl, gl * p, n), (1, 1, gl))
    rg = jnp.arange(gl * p)[:, None] // p
    cg = jnp.arange(gl * n)[None, :] // n
    return jnp.where(rg == cg, x, 0.0)


def _s5_mixer(proj, lam_re, lam_im, log_step, b_re, b_im, c_re, c_im, d_skip, l, cb, n_hs):
    t = S5_T
    g, n = lam_re.shape
    nl = g * S5_GROUP // LANES
    half = (g // nl) * n
    rb = cb * t
    brow = jnp.concatenate([_s5_group_rows(b_re.transpose(0, 2, 1), nl),
                            _s5_group_rows(b_im.transpose(0, 2, 1), nl)], axis=-1)
    crow = jnp.concatenate([_s5_group_rows(c_re, nl), _s5_group_rows(c_im, nl)], axis=-1)
    lam = jnp.stack([lam_re.astype(F32).reshape(nl, half), lam_im.astype(F32).reshape(nl, half),
                     jnp.repeat(log_step.astype(F32), n).reshape(nl, half)], axis=1)
    dsk = d_skip.astype(F32).reshape(nl, 1, LANES)
    kern = functools.partial(_s5_kernel, cb=cb, n_hs=n_hs)
    return pl.pallas_call(
        kern,
        grid=(nl, l // rb),
        in_specs=[pl.BlockSpec((rb, LANES), lambda a, b: (b, a)),
                  pl.BlockSpec((1, LANES, 2 * half), lambda a, b: (a, 0, 0)),
                  pl.BlockSpec((1, LANES, 2 * half), lambda a, b: (a, 0, 0)),
                  pl.BlockSpec((1, 3, half), lambda a, b: (a, 0, 0)),
                  pl.BlockSpec((1, 1, LANES), lambda a, b: (a, 0, 0))],
        out_specs=pl.BlockSpec((rb, LANES), lambda a, b: (b, a)),
        out_shape=jax.ShapeDtypeStruct((l, nl * LANES), BF16),
        scratch_shapes=[pltpu.VMEM((t * LANES, t * LANES), BF16),
                        pltpu.VMEM((t * LANES, 2 * half), BF16),
                        pltpu.VMEM((2 * half, t * LANES), BF16),
                        pltpu.VMEM((n_hs, 2 * half), F32),
                        pltpu.VMEM((rb, LANES), F32),
                        pltpu.VMEM((cb, t * LANES), BF16),
                        pltpu.VMEM((rb, LANES), F32),
                        pltpu.VMEM((1, 2 * half), F32)],
        compiler_params=_params(("arbitrary", "arbitrary")),
        name="s5_mixer",
    )(proj, brow, crow, lam, dsk)


def _unit_lower_inverse(lows):
    c = lows[0][0].shape[0]
    nb = len(lows[0])
    r = lax.broadcasted_iota(jnp.int32, (c, nb * c), 0)
    q = lax.broadcasted_iota(jnp.int32, (c, nb * c), 1)
    eye_w = jnp.where(q % c == r, 1.0, 0.0)
    rr = lax.broadcasted_iota(jnp.int32, (nb * c, nb * c), 0)
    qq = lax.broadcasted_iota(jnp.int32, (nb * c, nb * c), 1)
    on_diag = (rr // c) == (qq // c)

    def block_diag(xw):
        return jnp.where(on_diag, jnp.concatenate([xw] * nb, axis=0), 0.0).astype(BF16)

    def mm(xw, yw):
        return _dot(xw.astype(BF16), block_diag(yw))

    qc = q % c
    base = INV_BASE
    lws = [jnp.concatenate(group, axis=1) for group in lows]
    diag = [jnp.where(r // base == qc // base, lw, 0.0) for lw in lws]
    accs = [eye_w - d for d in diag]
    xs = [mm(d, d) for d in diag]
    steps = int(math.log2(base)) - 1
    for k in range(steps):
        last = k == steps - 1
        lhs = [acc if last else jnp.concatenate([x, acc], axis=0) for x, acc in zip(xs, accs)]
        both = [mm(a, x) for a, x in zip(lhs, xs)]
        accs = [acc + (b if last else b[c:]) for acc, b in zip(accs, both)]
        xs = [b[:c] for b in both]
    size = base
    while size < c:
        below = (r // (2 * size) == qc // (2 * size)) & (r // size != qc // size)
        subs = [jnp.where(below, lw, 0.0) for lw in lws]
        ys = [mm(s, acc) for s, acc in zip(subs, accs)]
        accs = [acc - mm(acc, y) for acc, y in zip(accs, ys)]
        size *= 2
    return [[acc[:, i * c:(i + 1) * c] for i in range(nb)] for acc in accs]


def _gdn_kernel(q_ref, k_ref, v_ref, z_ref, ab_ref, cw_ref, al_ref, dtb_ref, nw_ref, o_ref,
                xe_ref, qn_ref, kn_ref, vc_ref, gc_ref, beta_ref, s_ref,
                uv_ref, wk_ref, qd_ref, kd_ref, at_ref, gl_ref, *, rb):
    hd = GDN_HEAD_DIM
    nh = GDN_V_HEADS
    nqk = nh // 2
    qw = nqk * hd
    c = CHUNK
    halo = 8

    @pl.when(pl.program_id(0) == 0)
    def _():
        s_ref[...] = jnp.zeros_like(s_ref)
        xe_ref[0:halo, :] = jnp.zeros((halo, xe_ref.shape[1]), F32)

    xe_ref[halo:, 0:qw] = q_ref[...].astype(F32)
    xe_ref[halo:, qw:2 * qw] = k_ref[...].astype(F32)
    xe_ref[halo:, 2 * qw:] = v_ref[...].astype(F32)
    acc = cw_ref[GDN_CONV - 1:GDN_CONV, :] * xe_ref[halo:, :]
    for s in range(1, GDN_CONV):
        acc = acc + cw_ref[GDN_CONV - 1 - s:GDN_CONV - s, :] * xe_ref[halo - s:halo - s + rb, :]
    xe_ref[0:halo, :] = xe_ref[rb:rb + halo, :]
    qkv = _silu(acc)
    for p in range(nqk):
        qp = qkv[:, p * hd:(p + 1) * hd]
        qn_ref[:, p * hd:(p + 1) * hd] = qp * lax.rsqrt(jnp.sum(qp * qp, -1, keepdims=True) + EPS) * (hd ** -0.5)
        kp = qkv[:, qw + p * hd:qw + (p + 1) * hd]
        kn_ref[:, p * hd:(p + 1) * hd] = kp * lax.rsqrt(jnp.sum(kp * kp, -1, keepdims=True) + EPS)
    vc_ref[...] = qkv[:, 2 * qw:]

    ab = ab_ref[...]
    g = -jnp.exp(al_ref[...]) * jax.nn.softplus(ab + dtb_ref[...])
    beta_ref[...] = jax.nn.sigmoid(ab)
    row = lax.broadcasted_iota(jnp.int32, (rb, LANES), 0) % c
    for k in range(int(math.log2(c))):
        g = g + jnp.where(row >= (1 << k), pltpu.roll(g, 1 << k, 0), 0.0)
    gc_ref[...] = g

    ri = lax.broadcasted_iota(jnp.int32, (c, c), 0)
    ci = lax.broadcasted_iota(jnp.int32, (c, c), 1)
    causal = ri >= ci
    strict = ri > ci
    nw = nw_ref[...]

    def prepare_body(it, carry):
        rows = [pl.multiple_of((it * PREP + j) * c, c) for j in range(PREP)]
        scores = []
        for j, r0 in enumerate(rows):
            gcc = gc_ref[pl.ds(r0, c), :]
            gl_ref[it * PREP + j] = jnp.broadcast_to(jnp.exp(gcc[c - 1:c, :]), (8, LANES))
            for p in range(nqk):
                qp = qn_ref[pl.ds(r0, c), p * hd:(p + 1) * hd]
                kp = kn_ref[pl.ds(r0, c), p * hd:(p + 1) * hd]
                qk_kk = _dot_nt(jnp.concatenate([qp, kp], axis=0).astype(BF16), kp.astype(BF16))
                scores.append((r0, p, gcc, qp, kp, qk_kk))
        lows, rhss, where = [], [], []
        for r0, p, gcc, qp, kp, qk_kk in scores:
            gct = gcc.T
            bet = beta_ref[pl.ds(r0, c), :]
            for h in (2 * p, 2 * p + 1):
                cols = slice(h * hd, (h + 1) * hd)
                gcol = jnp.broadcast_to(gcc[:, h:h + 1], (c, hd))
                bcol = jnp.broadcast_to(bet[:, nh + h:nh + h + 1], (c, hd))
                dec = jnp.exp(jnp.where(causal, gcol[:, :c] - gct[h:h + 1, :], -jnp.inf))
                lows.append(jnp.where(strict, bcol[:, :c] * qk_kk[c:] * dec, 0.0))
                at_ref[pl.ds(r0, c), h * hd:h * hd + c] = (qk_kk[:c] * dec).astype(BF16)
                eg = jnp.exp(gcol)
                qd_ref[pl.ds(r0, c), cols] = (qp * eg).astype(BF16)
                kd_ref[pl.ds(r0, c), cols] = (kp * jnp.exp(gcc[c - 1:c, h:h + 1] - gcol)).astype(BF16)
                rhss.append(jnp.concatenate([vc_ref[pl.ds(r0, c), cols] * bcol, kp * (bcol * eg)],
                                            axis=1).astype(BF16))
                where.append((r0, cols))
        t_groups = _unit_lower_inverse([lows[i:i + 4] for i in range(0, len(lows), 4)])
        t_mats = [t for group in t_groups for t in group]
        uws = [_dot(t.astype(BF16), rhs) for t, rhs in zip(t_mats, rhss)]
        for uw, (r0, cols) in zip(uws, where):
            uv_ref[pl.ds(r0, c), cols] = uw[:, :hd]
            wk_ref[pl.ds(r0, c), cols] = uw[:, hd:].astype(BF16)
        return carry

    lax.fori_loop(0, rb // (PREP * c), prepare_body, 0)

    def state_body(n, carry):
        r0 = pl.multiple_of(n * c, c)
        gl = gl_ref[n]
        heads = [slice(h * hd, (h + 1) * hd) for h in range(nh)]
        sts = [s_ref[h] for h in range(nh)]
        ws_qs = [_dot(jnp.concatenate([wk_ref[pl.ds(r0, c), cols], qd_ref[pl.ds(r0, c), cols]], axis=0),
                      st.astype(BF16)) for cols, st in zip(heads, sts)]
        v16s = [(uv_ref[pl.ds(r0, c), cols] - wq[:c]).astype(BF16) for cols, wq in zip(heads, ws_qs)]
        upd = [_dot_tn(kd_ref[pl.ds(r0, c), cols], v16) for cols, v16 in zip(heads, v16s)]
        intra = [_dot(at_ref[pl.ds(r0, c), h * hd:h * hd + c], v16) for h, v16 in enumerate(v16s)]
        for h, cols in enumerate(heads):
            s_ref[h] = sts[h] * gl[0:1, h:h + 1] + upd[h]
            o = ws_qs[h][c:] + intra[h]
            zh = z_ref[pl.ds(r0, c), cols].astype(F32)
            on = o * lax.rsqrt(jnp.mean(o * o, -1, keepdims=True) + EPS) * nw
            o_ref[pl.ds(r0, c), cols] = (on * _silu(zh)).astype(o_ref.dtype)
        return carry

    lax.fori_loop(0, rb // c, state_body, 0, unroll=8)


def _gdn_mixer(proj, ab, conv_w, a_log, dt_bias, norm_w, l, rb):
    hd, nh = GDN_HEAD_DIM, GDN_V_HEADS
    qw = (nh // 2) * hd
    vw = nh * hd
    pad = LANES - nh
    al = jnp.pad(a_log.astype(F32), (0, pad)).reshape(1, LANES)
    dtb = jnp.pad(dt_bias.astype(F32), (0, pad)).reshape(1, LANES)
    kern = functools.partial(_gdn_kernel, rb=rb)
    return pl.pallas_call(
        kern,
        grid=(l // rb,),
        in_specs=[pl.BlockSpec((rb, qw), lambda i: (i, 2)),
                  pl.BlockSpec((rb, qw), lambda i: (i, 3)),
                  pl.BlockSpec((rb, vw), lambda i: (i, 2)),
                  pl.BlockSpec((rb, vw), lambda i: (i, 3)),
                  pl.BlockSpec((rb, LANES), lambda i: (i, 0)),
                  pl.BlockSpec((GDN_CONV, 2 * qw + vw), lambda i: (0, 0)),
                  pl.BlockSpec((1, LANES), lambda i: (0, 0)),
                  pl.BlockSpec((1, LANES), lambda i: (0, 0)),
                  pl.BlockSpec((1, hd), lambda i: (0, 0))],
        out_specs=pl.BlockSpec((rb, vw), lambda i: (i, 0)),
        out_shape=jax.ShapeDtypeStruct((l, vw), BF16),
        scratch_shapes=[pltpu.VMEM((rb + 8, 2 * qw + vw), F32),
                        pltpu.VMEM((rb, qw), F32),
                        pltpu.VMEM((rb, qw), F32),
                        pltpu.VMEM((rb, vw), F32),
                        pltpu.VMEM((rb, LANES), F32),
                        pltpu.VMEM((rb, LANES), F32),
                        pltpu.VMEM((nh, hd, hd), F32),
                        pltpu.VMEM((rb, vw), F32),
                        pltpu.VMEM((rb, vw), BF16),
                        pltpu.VMEM((rb, vw), BF16),
                        pltpu.VMEM((rb, vw), BF16),
                        pltpu.VMEM((rb, vw), BF16),
                        pltpu.VMEM((rb // CHUNK, 8, LANES), F32)],
        compiler_params=_params(("arbitrary",)),
        name="gdn_mixer",
    )(proj, proj, proj, proj, ab, conv_w.astype(F32), al, dtb, norm_w.astype(F32).reshape(1, hd))


def _gla_kernel(q_ref, k_ref, v_ref, r_ref, gl_ref, w2_ref, gb_ref, nw_ref, o_ref,
                qt_ref, kt_ref, gle_ref, st_ref, *, rb):
    nh = GLA_HEADS
    dk = q_ref.shape[1] // nh
    dv = v_ref.shape[1] // nh
    c = CHUNK

    @pl.when(pl.program_id(0) == 0)
    def _():
        st_ref[...] = jnp.zeros_like(st_ref)

    x = _dot_split(gl_ref[...], w2_ref[...]) + gb_ref[...]
    b = jax.nn.log_sigmoid(x) / GLA_TAU
    row = lax.broadcasted_iota(jnp.int32, b.shape, 0) % c
    for k in range(int(math.log2(c))):
        b = b + jnp.where(row >= (1 << k), pltpu.roll(b, 1 << k, 0), 0.0)
    q = q_ref[...].astype(F32) * (dk ** -0.5)
    kf = k_ref[...].astype(F32)
    qt_ref[...] = (q * jnp.exp(b)).astype(BF16)
    kt_ref[...] = (kf * jnp.exp(-b)).astype(BF16)
    for n in range(rb // c):
        gle_ref[n] = jnp.broadcast_to(jnp.exp(b[(n + 1) * c - 1:(n + 1) * c, :]), (8, b.shape[1]))

    ri = lax.broadcasted_iota(jnp.int32, (c, c), 0)
    ci = lax.broadcasted_iota(jnp.int32, (c, c), 1)
    causal = ri >= ci
    nw = nw_ref[...]

    def chunk_body(it, carry):
        sts = [st_ref[h] for h in range(nh)]
        work = []
        for j in range(GLA_GROUP):
            n = it * GLA_GROUP + j
            r0 = pl.multiple_of(n * c, c)
            qts = [qt_ref[pl.ds(r0, c), h * dk:(h + 1) * dk] for h in range(nh)]
            kts = [kt_ref[pl.ds(r0, c), h * dk:(h + 1) * dk] for h in range(nh)]
            vhs = [v_ref[pl.ds(r0, c), h * dv:(h + 1) * dv] for h in range(nh)]
            scores = [_dot_nt(qts[h], kts[h]) for h in range(nh)]
            upd = [_dot_tn(vhs[h], kts[h]) for h in range(nh)]
            work.append((n, r0, qts, vhs, scores, upd))
        for n, r0, qts, vhs, scores, upd in work:
            gle = gle_ref[n]
            inter = [_dot_nt(qts[h], sts[h].astype(BF16)) for h in range(nh)]
            intra = [_dot(jnp.where(causal, scores[h], 0.0).astype(BF16), vhs[h]) for h in range(nh)]
            sts = [(sts[h] + upd[h]) * gle[0:1, h * dk:(h + 1) * dk] for h in range(nh)]
            for h in range(nh):
                o = intra[h] + inter[h]
                rh = r_ref[pl.ds(r0, c), h * dv:(h + 1) * dv].astype(F32)
                on = o * lax.rsqrt(jnp.mean(o * o, -1, keepdims=True) + EPS) * nw
                o_ref[pl.ds(r0, c), h * dv:(h + 1) * dv] = (on * _silu(rh)).astype(o_ref.dtype)
        for h in range(nh):
            st_ref[h] = sts[h]
        return carry

    lax.fori_loop(0, rb // (GLA_GROUP * c), chunk_body, 0, unroll=True)


def _gla_mixer(proj, g_low, gate_w2, gate_b, norm_w, l, rb, dk_all, dv_all):
    nh = GLA_HEADS
    lowrank = gate_w2.shape[0]
    w2 = jnp.pad(gate_w2.astype(F32), ((0, LANES - lowrank), (0, 0)))
    kern = functools.partial(_gla_kernel, rb=rb)
    return pl.pallas_call(
        kern,
        grid=(l // rb,),
        in_specs=[pl.BlockSpec((rb, dk_all), lambda i: (i, 0)),
                  pl.BlockSpec((rb, dk_all), lambda i: (i, 1)),
                  pl.BlockSpec((rb, dv_all), lambda i: (i, 1)),
                  pl.BlockSpec((rb, dv_all), lambda i: (i, 2)),
                  pl.BlockSpec((rb, LANES), lambda i: (i, 0)),
                  pl.BlockSpec((LANES, dk_all), lambda i: (0, 0)),
                  pl.BlockSpec((1, dk_all), lambda i: (0, 0)),
                  pl.BlockSpec((1, dv_all // nh), lambda i: (0, 0))],
        out_specs=pl.BlockSpec((rb, dv_all), lambda i: (i, 0)),
        out_shape=jax.ShapeDtypeStruct((l, dv_all), BF16),
        scratch_shapes=[pltpu.VMEM((rb, dk_all), BF16),
                        pltpu.VMEM((rb, dk_all), BF16),
                        pltpu.VMEM((rb // CHUNK, 8, dk_all), F32),
                        pltpu.VMEM((nh, dv_all // nh, dk_all // nh), F32)],
        compiler_params=_params(("arbitrary",)),
        name="gla_mixer",
    )(proj, proj, proj, proj, g_low, w2, gate_b.astype(F32).reshape(1, dk_all),
      norm_w.astype(F32).reshape(1, dv_all // nh))


def _out0_kernel(x_ref, ya_ref, yb_ref, gw_ref, gb_ref, wa_ref, wb_ref, pw_ref, gt_ref, o_ref):
    ya = ya_ref[...]
    gate = jax.nn.sigmoid(_dot(ya, gw_ref[...]) + gb_ref[...])
    ya = (ya.astype(F32) * gate).astype(BF16)
    m = _dot(ya, wa_ref[...]) + _dot(yb_ref[...], wb_ref[...])
    o_ref[...] = _postnorm_residual(x_ref[...], m, pw_ref[...], gt_ref[...])


def _out1_kernel(x_ref, y_ref, w_ref, pw_ref, gt_ref, o_ref):
    m = _dot(y_ref[...], w_ref[...])
    o_ref[...] = _postnorm_residual(x_ref[...], m, pw_ref[...], gt_ref[...])


def _const_spec(shape):
    return pl.BlockSpec(shape, lambda i: (0,) * len(shape), pipeline_mode=pl.Buffered(1))


def _out_project0(x, ya, yb, glu_w, glu_b, w_out, post_w, mod, tm):
    l, d = x.shape
    wa = ya.shape[1]
    wb = yb.shape[1]
    return pl.pallas_call(
        _out0_kernel,
        grid=(l // tm,),
        in_specs=[pl.BlockSpec((tm, d), lambda i: (i, 0)),
                  pl.BlockSpec((tm, wa), lambda i: (i, 0)),
                  pl.BlockSpec((tm, wb), lambda i: (i, 0)),
                  _const_spec((wa, wa)),
                  _const_spec((1, wa)),
                  pl.BlockSpec((wa, d), lambda i: (0, 0), pipeline_mode=pl.Buffered(1)),
                  pl.BlockSpec((wb, d), lambda i: (1, 0), pipeline_mode=pl.Buffered(1)),
                  _const_spec((1, d)),
                  pl.BlockSpec((1, d), lambda i: (0, 2))],
        out_specs=pl.BlockSpec((tm, d), lambda i: (i, 0)),
        out_shape=jax.ShapeDtypeStruct((l, d), F32),
        compiler_params=_params(("arbitrary",)),
        name="out_project0",
    )(x, ya, yb, glu_w, glu_b.astype(F32).reshape(1, wa), w_out, w_out, post_w.reshape(1, d), mod)


def _out_project1(x, y, w_out, post_w, mod, tm):
    l, d = x.shape
    w = y.shape[1]
    return pl.pallas_call(
        _out1_kernel,
        grid=(l // tm,),
        in_specs=[pl.BlockSpec((tm, d), lambda i: (i, 0)),
                  pl.BlockSpec((tm, w), lambda i: (i, 0)),
                  _const_spec((w, d)),
                  _const_spec((1, d)),
                  pl.BlockSpec((1, d), lambda i: (0, 2))],
        out_specs=pl.BlockSpec((tm, d), lambda i: (i, 0)),
        out_shape=jax.ShapeDtypeStruct((l, d), F32),
        compiler_params=_params(("arbitrary",)),
        name="out_project1",
    )(x, y, w_out, post_w.reshape(1, d), mod)


def _ffn_kernel(x_ref, nw_ref, sc_ref, sh_ref, wg_ref, wu_ref, wd_ref, pw_ref, gt_ref, o_ref,
                h_ref, a_ref, *, n_hid, n_out):
    j = pl.program_id(1)
    th = a_ref.shape[2]
    tn = wd_ref.shape[1]

    @pl.when(j == 0)
    def _():
        for r in _row_slabs(x_ref.shape[0]):
            h_ref[r, :] = _prenorm(x_ref[r, :], nw_ref[...], sc_ref[...], sh_ref[...]).astype(BF16)

    @pl.when(j < n_hid)
    def _():
        h = h_ref[...]
        a_ref[j] = (_silu(_dot(h, wg_ref[...])) * _dot(h, wu_ref[...])).astype(BF16)

    for jj in range(n_out):
        @pl.when(j == n_hid + jj)
        def _():
            acc = _dot(a_ref[0], wd_ref[0:th, :])
            for k in range(1, n_hid):
                acc = acc + _dot(a_ref[k], wd_ref[k * th:(k + 1) * th, :])
            o_ref[:, jj * tn:(jj + 1) * tn] = acc

    @pl.when(j == n_hid + n_out - 1)
    def _():
        for r in _row_slabs(x_ref.shape[0]):
            o_ref[r, :] = _postnorm_residual(x_ref[r, :], o_ref[r, :], pw_ref[...], gt_ref[...])


def _ffn_block(x, mod, pre_w, post_w, w_gate, w_up, w_down, tm, th, tn):
    l, d = x.shape
    hid = w_gate.shape[1]
    n_hid, n_out = hid // th, d // tn
    kern = functools.partial(_ffn_kernel, n_hid=n_hid, n_out=n_out)
    return pl.pallas_call(
        kern,
        grid=(l // tm, n_hid + n_out),
        in_specs=[pl.BlockSpec((tm, d), lambda i, j: (i, 0)),
                  pl.BlockSpec((1, d), lambda i, j: (0, 0)),
                  pl.BlockSpec((1, d), lambda i, j: (0, 4)),
                  pl.BlockSpec((1, d), lambda i, j: (0, 3)),
                  pl.BlockSpec((d, th), lambda i, j: (0, jnp.minimum(j, n_hid - 1))),
                  pl.BlockSpec((d, th), lambda i, j: (0, jnp.minimum(j, n_hid - 1))),
                  pl.BlockSpec((hid, tn), lambda i, j: (0, jnp.maximum(j - n_hid, 0))),
                  pl.BlockSpec((1, d), lambda i, j: (0, 0)),
                  pl.BlockSpec((1, d), lambda i, j: (0, 5))],
        out_specs=pl.BlockSpec((tm, d), lambda i, j: (i, 0)),
        out_shape=jax.ShapeDtypeStruct((l, d), F32),
        scratch_shapes=[pltpu.VMEM((tm, d), BF16),
                        pltpu.VMEM((n_hid, tm, th), BF16)],
        compiler_params=_params(("arbitrary", "arbitrary")),
        name="ffn_block",
    )(x, pre_w.reshape(1, d), mod, mod, w_gate, w_up, w_down, post_w.reshape(1, d), mod)


def kernel(x, c, ada_w0, ada_b0, mix_pre0, mix_post0, ffn_pre0, ffn_post0, w_in0, s5_lambda_re, s5_lambda_im, s5_log_step, s5_b_re, s5_b_im, s5_c_re, s5_c_im, s5_d, s5_glu_w, s5_glu_b, gdn_conv_w, gdn_a_log, gdn_dt_bias, gdn_norm_w, w_out0, ffn_gate0, ffn_up0, ffn_down0, ada_w1, ada_b1, mix_pre1, mix_post1, ffn_pre1, ffn_post1, w_in1, gla_gate_w2, gla_gate_b, gla_norm_w, w_out1, ffn_gate1, ffn_up1, ffn_down1):
    bsz, l, d = x.shape
    assert bsz == 1
    x = x.reshape(l, d)
    tm = min(512, l)
    tp = min(1024, l)
    tf = min(512, l)
    th = 512
    tn = 512
    rb = min(512, l)
    cb = min(512, l // S5_T)
    n_hs = int(math.log2(cb))
    assert l % tm == 0 and l % rb == 0 and l % (cb * S5_T) == 0 and (1 << n_hs) == cb

    s5_w = s5_glu_w.shape[0]
    qk_w = (GDN_V_HEADS // 2) * GDN_HEAD_DIM
    v_w = GDN_V_HEADS * GDN_HEAD_DIM
    n0 = s5_w + 2 * qk_w + 2 * v_w
    dk_all = gla_gate_w2.shape[1]
    dv_all = w_out1.shape[0]
    n1 = 2 * dk_all + 2 * dv_all

    mod0 = _ada_modulation(c, ada_w0, ada_b0)
    mod1 = _ada_modulation(c, ada_w1, ada_b1)
    tc = 1024
    proj0, ab = _prenorm_project(x, mod0, 0, mix_pre0, w_in0, n0, tp, tc)
    y_a = _s5_mixer(proj0, s5_lambda_re, s5_lambda_im, s5_log_step, s5_b_re, s5_b_im, s5_c_re, s5_c_im, s5_d,
                    l, cb, n_hs)
    y_b = _gdn_mixer(proj0, ab, gdn_conv_w, gdn_a_log, gdn_dt_bias, gdn_norm_w, l, rb)
    x = _out_project0(x, y_a, y_b, s5_glu_w.astype(BF16), s5_glu_b, w_out0.astype(BF16), mix_post0, mod0, tm)
    x = _ffn_block(x, mod0, ffn_pre0, ffn_post0, ffn_gate0.astype(BF16), ffn_up0.astype(BF16),
                   ffn_down0.astype(BF16), tf, th, tn)

    proj1, g_low = _prenorm_project(x, mod1, 0, mix_pre1, w_in1, n1, tp, tc)
    y_c = _gla_mixer(proj1, g_low, gla_gate_w2, gla_gate_b, gla_norm_w, l, rb, dk_all, dv_all)
    x = _out_project1(x, y_c, w_out1.astype(BF16), mix_post1, mod1, tm)
    x = _ffn_block(x, mod1, ffn_pre1, ffn_post1, ffn_gate1.astype(BF16), ffn_up1.astype(BF16),
                   ffn_down1.astype(BF16), tf, th, tn)
    return x.reshape(bsz, l, d)
```

```python
import functools
import math
from typing import NamedTuple

import jax
import jax.numpy as jnp
from jax import lax
from jax.experimental import pallas as pl
from jax.experimental.pallas import tpu as pltpu

F32 = jnp.float32
BF16 = jnp.bfloat16
EPS = 1e-6

LANES = 128
MXU_N = 256
VMEM_LIMIT = 56 << 20

S5_GROUP = 16
S5_STATE = 64
S5_T = 16
GDN_HEAD_DIM = 128
GDN_V_HEADS = 8
GDN_CONV = 4
CHUNK = 64
INV_BASE = 16
PREP = 8
GLA_HEADS = 4
GLA_TAU = 16.0
GLA_GROUP = 2


def _params(sem):
    return pltpu.CompilerParams(dimension_semantics=sem, vmem_limit_bytes=VMEM_LIMIT)


def _dot(a, b):
    return jnp.dot(a, b, preferred_element_type=F32)


def _dot_nt(a, b):
    return lax.dot_general(a, b, (((1,), (1,)), ((), ())), preferred_element_type=F32)


def _dot_tn(a, b):
    return lax.dot_general(a, b, (((0,), (0,)), ((), ())), preferred_element_type=F32)


def _split_bf16(x):
    hi = x.astype(BF16)
    lo = (x - hi.astype(F32)).astype(BF16)
    return hi, lo


def _dot_split(a, b):
    ah, al = _split_bf16(a)
    bh, bl = _split_bf16(b)
    return _dot(ah, bh) + (_dot(ah, bl) + _dot(al, bh))


def _silu(x):
    return x * jax.nn.sigmoid(x)


def _prenorm(x, nw, sc, sh):
    ms = jnp.mean(x * x, axis=-1, keepdims=True)
    return (x * lax.rsqrt(ms + EPS)) * (nw * (1.0 + sc)) + sh


def _postnorm_residual(x, m, pw, gt):
    ms = jnp.mean(m * m, axis=-1, keepdims=True)
    return x + (m * lax.rsqrt(ms + EPS)) * (gt * pw)


def _shift_rows(x, s, row):
    return jnp.where(row >= s, pltpu.roll(x, s, 0), 0.0)


def _ada_kernel(c_ref, w_ref, b_ref, o_ref):
    s = _silu(c_ref[...])
    o_ref[...] = jnp.sum(s * w_ref[...], axis=0, keepdims=True) + b_ref[...]


def _ada_modulation(c, w, b):
    d, n = w.shape
    tn = 1024
    return pl.pallas_call(
        _ada_kernel,
        grid=(n // tn,),
        in_specs=[pl.BlockSpec((d, 1), lambda j: (0, 0)),
                  pl.BlockSpec((d, tn), lambda j: (0, j)),
                  pl.BlockSpec((1, tn), lambda j: (0, j))],
        out_specs=pl.BlockSpec((1, tn), lambda j: (0, j)),
        out_shape=jax.ShapeDtypeStruct((1, n), F32),
        compiler_params=_params(("arbitrary",)),
        name="ada_modulation",
    )(c.reshape(d, 1), w, b.reshape(1, n))


def _proj_kernel(x_ref, nw_ref, sc_ref, sh_ref, w_ref, ws_ref, o_ref, os_ref, h_ref):
    @pl.when(pl.program_id(1) == 0)
    def _():
        hb = _prenorm(x_ref[...], nw_ref[...], sc_ref[...], sh_ref[...]).astype(BF16)
        h_ref[...] = hb
        os_ref[...] = _dot(hb, ws_ref[...])

    o_ref[...] = _dot(h_ref[...], w_ref[...]).astype(o_ref.dtype)


def _prenorm_project(x, mod, mod_base, norm_w, w_all, n, tm, tn):
    l, d = x.shape
    w_main = w_all.astype(BF16)
    w_small = jnp.pad(w_all[:, n:], ((0, 0), (0, LANES - (w_all.shape[1] - n)))).astype(BF16)
    ns = LANES
    return pl.pallas_call(
        _proj_kernel,
        grid=(l // tm, n // tn),
        in_specs=[pl.BlockSpec((tm, d), lambda i, j: (i, 0)),
                  pl.BlockSpec((1, d), lambda i, j: (0, 0)),
                  pl.BlockSpec((1, d), lambda i, j: (0, mod_base + 1)),
                  pl.BlockSpec((1, d), lambda i, j: (0, mod_base)),
                  pl.BlockSpec((d, tn), lambda i, j: (0, j)),
                  pl.BlockSpec((d, ns), lambda i, j: (0, 0))],
        out_specs=[pl.BlockSpec((tm, tn), lambda i, j: (i, j)),
                   pl.BlockSpec((tm, ns), lambda i, j: (i, 0))],
        out_shape=[jax.ShapeDtypeStruct((l, n), BF16), jax.ShapeDtypeStruct((l, ns), F32)],
        scratch_shapes=[pltpu.VMEM((tm, d), BF16)],
        compiler_params=_params(("arbitrary", "arbitrary")),
        name="prenorm_project",
    )(x, norm_w.reshape(1, d), mod, mod, w_main, w_small)


def _s5_build_tables(b_ref, c_ref, lam_ref, d_ref, wi_ref, wb_ref, wc_ref, hs_ref, n_hs):
    t = S5_T
    half = b_ref.shape[2] // 2
    lr, li = lam_ref[0, 0:1, :], lam_ref[0, 1:2, :]
    dt = jnp.exp(lam_ref[0, 2:3, :])
    mag = jnp.exp(lr * dt)
    ar, ai = mag * jnp.cos(li * dt), mag * jnp.sin(li * dt)
    den = lr * lr + li * li
    nr, ni = ar - 1.0, ai
    f_re = (nr * lr + ni * li) / den
    f_im = (ni * lr - nr * li) / den
    b_re, b_im = b_ref[0, :, :half], b_ref[0, :, half:]
    bb_re = f_re * b_re - f_im * b_im
    bb_im = f_re * b_im + f_im * b_re
    c_re, c_im = c_ref[0, :, :half], c_ref[0, :, half:]
    bb_hi, bb_lo = _split_bf16(jnp.concatenate([bb_re, bb_im], axis=1))
    rr = lax.broadcasted_iota(jnp.int32, (LANES, LANES), 0)
    qq = lax.broadcasted_iota(jnp.int32, (LANES, LANES), 1)
    skip = jnp.where(rr == qq, d_ref[0], 0.0)

    pr, pi = jnp.ones_like(ar), jnp.zeros_like(ar)
    pows = []
    for d in range(t + 1):
        pows.append((pr, pi))
        cs = jnp.concatenate([c_re * pr - c_im * pi, -(c_re * pi + c_im * pr)], axis=1)
        if d < t:
            cs_hi, cs_lo = _split_bf16(cs)
            k_d = _dot_nt(bb_hi, cs_hi) + (_dot_nt(bb_hi, cs_lo) + _dot_nt(bb_lo, cs_hi))
            if d == 0:
                k_d = k_d + skip
            k_d = k_d.astype(BF16)
            for s in range(t - d):
                wi_ref[s * LANES:(s + 1) * LANES, (s + d) * LANES:(s + d + 1) * LANES] = k_d
        if d >= 1:
            wc_ref[:, (d - 1) * LANES:d * LANES] = cs.T.astype(BF16)
        pr, pi = pr * ar - pi * ai, pr * ai + pi * ar
    for tb in range(t * LANES // MXU_N):
        wi_ref[(2 * tb + 1) * LANES:(2 * tb + 2) * LANES, 2 * tb * LANES:(2 * tb + 1) * LANES] = (
            jnp.zeros((LANES, LANES), BF16))
    for s in range(t):
        pr, pi = pows[t - 1 - s]
        wb_ref[s * LANES:(s + 1) * LANES, :half] = (bb_re * pr - bb_im * pi).astype(BF16)
        wb_ref[s * LANES:(s + 1) * LANES, half:] = (bb_re * pi + bb_im * pr).astype(BF16)
    pr, pi = pows[t]
    for k in range(n_hs):
        hs_ref[k:k + 1, :half] = pr
        hs_ref[k:k + 1, half:] = pi
        pr, pi = pr * pr - pi * pi, 2.0 * pr * pi


def _s5_kernel(u_ref, b_ref, c_ref, lam_ref, d_ref, o_ref,
               wi_ref, wb_ref, wc_ref, hs_ref, uf_ref, u2_ref, yf_ref, carry_ref, *, cb, n_hs):
    t = S5_T
    half = carry_ref.shape[1] // 2

    @pl.when(pl.program_id(1) == 0)
    def _():
        carry_ref[...] = jnp.zeros_like(carry_ref)
        _s5_build_tables(b_ref, c_ref, lam_ref, d_ref, wi_ref, wb_ref, wc_ref, hs_ref, n_hs)

    uf_ref[...] = u_ref[...].astype(F32)
    for k in range(t):
        u2_ref[:, k * LANES:(k + 1) * LANES] = uf_ref[pl.ds(k, cb, stride=t), :].astype(BF16)
    u2 = u2_ref[...]

    s = _dot(u2, wb_ref[...])
    n_tb = t * LANES // MXU_N
    intra = [_dot(u2[:, :(tb + 1) * MXU_N], wi_ref[:(tb + 1) * MXU_N, tb * MXU_N:(tb + 1) * MXU_N])
             for tb in range(n_tb)]
    sr, si = s[:, :half], s[:, half:]
    row = lax.broadcasted_iota(jnp.int32, (cb, half), 0)
    cr, ci = carry_ref[:, :half], carry_ref[:, half:]
    pr, pi = hs_ref[0:1, :half], hs_ref[0:1, half:]
    sr = sr + jnp.where(row == 0, pr * cr - pi * ci, 0.0)
    si = si + jnp.where(row == 0, pr * ci + pi * cr, 0.0)
    for k in range(n_hs):
        pr, pi = hs_ref[k:k + 1, :half], hs_ref[k:k + 1, half:]
        shr, shi = _shift_rows(sr, 1 << k, row), _shift_rows(si, 1 << k, row)
        sr, si = sr + pr * shr - pi * shi, si + pr * shi + pi * shr
    epr = jnp.where(row >= 1, pltpu.roll(sr, 1, 0), cr)
    epi = jnp.where(row >= 1, pltpu.roll(si, 1, 0), ci)
    carry_ref[:, :half] = sr[cb - 1:cb]
    carry_ref[:, half:] = si[cb - 1:cb]
    ep = jnp.concatenate([epr, epi], axis=1).astype(BF16)
    y_inter = _dot(ep, wc_ref[...])

    for tb in range(n_tb):
        yb = jax.nn.gelu(y_inter[:, tb * MXU_N:(tb + 1) * MXU_N] + intra[tb], approximate=True)
        for q in range(MXU_N // LANES):
            yf_ref[pl.ds(tb * (MXU_N // LANES) + q, cb, stride=t), :] = yb[:, q * LANES:(q + 1) * LANES]
    o_ref[...] = yf_ref[...].astype(o_ref.dtype)


def _s5_group_rows(x, nl):
    g, p, n = x.shape
    gl = g // nl
    x = jnp.tile(x.astype(F32).reshape(nl, gl * p, n), (1, 1, gl))
    rg = jnp.arange(gl * p)[:, None] // p
    cg = jnp.arange(gl * n)[None, :] // n
    return jnp.where(rg == cg, x, 0.0)


def _s5_mixer(proj, lam_re, lam_im, log_step, b_re, b_im, c_re, c_im, d_skip, l, cb, n_hs):
    t = S5_T
    g, n = lam_re.shape
    nl = g * S5_GROUP // LANES
    half = (g // nl) * n
    rb = cb * t
    brow = jnp.concatenate([_s5_group_rows(b_re.transpose(0, 2, 1), nl),
                            _s5_group_rows(b_im.transpose(0, 2, 1), nl)], axis=-1)
    crow = jnp.concatenate([_s5_group_rows(c_re, nl), _s5_group_rows(c_im, nl)], axis=-1)
    lam = jnp.stack([lam_re.astype(F32).reshape(nl, half), lam_im.astype(F32).reshape(nl, half),
                     jnp.repeat(log_step.astype(F32), n).reshape(nl, half)], axis=1)
    dsk = d_skip.astype(F32).reshape(nl, 1, LANES)
    kern = functools.partial(_s5_kernel, cb=cb, n_hs=n_hs)
    return pl.pallas_call(
        kern,
        grid=(nl, l // rb),
        in_specs=[pl.BlockSpec((rb, LANES), lambda a, b: (b, a)),
                  pl.BlockSpec((1, LANES, 2 * half), lambda a, b: (a, 0, 0)),
                  pl.BlockSpec((1, LANES, 2 * half), lambda a, b: (a, 0, 0)),
                  pl.BlockSpec((1, 3, half), lambda a, b: (a, 0, 0)),
                  pl.BlockSpec((1, 1, LANES), lambda a, b: (a, 0, 0))],
        out_specs=pl.BlockSpec((rb, LANES), lambda a, b: (b, a)),
        out_shape=jax.ShapeDtypeStruct((l, nl * LANES), BF16),
        scratch_shapes=[pltpu.VMEM((t * LANES, t * LANES), BF16),
                        pltpu.VMEM((t * LANES, 2 * half), BF16),
                        pltpu.VMEM((2 * half, t * LANES), BF16),
                        pltpu.VMEM((n_hs, 2 * half), F32),
                        pltpu.VMEM((rb, LANES), F32),
                        pltpu.VMEM((cb, t * LANES), BF16),
                        pltpu.VMEM((rb, LANES), F32),
                        pltpu.VMEM((1, 2 * half), F32)],
        compiler_params=_params(("arbitrary", "arbitrary")),
        name="s5_mixer",
    )(proj, brow, crow, lam, dsk)


def _unit_lower_inverse(lows):
    c = lows[0][0].shape[0]
    nb = len(lows[0])
    r = lax.broadcasted_iota(jnp.int32, (c, nb * c), 0)
    q = lax.broadcasted_iota(jnp.int32, (c, nb * c), 1)
    eye_w = jnp.where(q % c == r, 1.0, 0.0)
    rr = lax.broadcasted_iota(jnp.int32, (nb * c, nb * c), 0)
    qq = lax.broadcasted_iota(jnp.int32, (nb * c, nb * c), 1)
    on_diag = (rr // c) == (qq // c)

    def block_diag(xw):
        return jnp.where(on_diag, jnp.concatenate([xw] * nb, axis=0), 0.0).astype(BF16)

    def mm(xw, yw):
        return _dot(xw.astype(BF16), block_diag(yw))

    qc = q % c
    base = INV_BASE
    lws = [jnp.concatenate(group, axis=1) for group in lows]
    diag = [jnp.where(r // base == qc // base, lw, 0.0) for lw in lws]
    accs = [eye_w - d for d in diag]
    xs = [mm(d, d) for d in diag]
    steps = int(math.log2(base)) - 1
    for k in range(steps):
        last = k == steps - 1
        lhs = [acc if last else jnp.concatenate([x, acc], axis=0) for x, acc in zip(xs, accs)]
        both = [mm(a, x) for a, x in zip(lhs, xs)]
        accs = [acc + (b if last else b[c:]) for acc, b in zip(accs, both)]
        xs = [b[:c] for b in both]
    size = base
    while size < c:
        below = (r // (2 * size) == qc // (2 * size)) & (r // size != qc // size)
        subs = [jnp.where(below, lw, 0.0) for lw in lws]
        ys = [mm(s, acc) for s, acc in zip(subs, accs)]
        accs = [acc - mm(acc, y) for acc, y in zip(accs, ys)]
        size *= 2
    return [[acc[:, i * c:(i + 1) * c] for i in range(nb)] for acc in accs]


def _gdn_kernel(q_ref, k_ref, v_ref, z_ref, ab_ref, cw_ref, al_ref, dtb_ref, nw_ref, o_ref,
                xe_ref, qn_ref, kn_ref, vc_ref, gc_ref, beta_ref, s_ref,
                uv_ref, wk_ref, qd_ref, kd_ref, at_ref, gl_ref, *, rb):
    hd = GDN_HEAD_DIM
    nh = GDN_V_HEADS
    nqk = nh // 2
    qw = nqk * hd
    c = CHUNK
    halo = 8

    @pl.when(pl.program_id(0) == 0)
    def _():
        s_ref[...] = jnp.zeros_like(s_ref)
        xe_ref[0:halo, :] = jnp.zeros((halo, xe_ref.shape[1]), F32)

    xe_ref[halo:, 0:qw] = q_ref[...].astype(F32)
    xe_ref[halo:, qw:2 * qw] = k_ref[...].astype(F32)
    xe_ref[halo:, 2 * qw:] = v_ref[...].astype(F32)
    acc = cw_ref[GDN_CONV - 1:GDN_CONV, :] * xe_ref[halo:, :]
    for s in range(1, GDN_CONV):
        acc = acc + cw_ref[GDN_CONV - 1 - s:GDN_CONV - s, :] * xe_ref[halo - s:halo - s + rb, :]
    xe_ref[0:halo, :] = xe_ref[rb:rb + halo, :]
    qkv = _silu(acc)
    for p in range(nqk):
        qp = qkv[:, p * hd:(p + 1) * hd]
        qn_ref[:, p * hd:(p + 1) * hd] = qp * lax.rsqrt(jnp.sum(qp * qp, -1, keepdims=True) + EPS) * (hd ** -0.5)
        kp = qkv[:, qw + p * hd:qw + (p + 1) * hd]
        kn_ref[:, p * hd:(p + 1) * hd] = kp * lax.rsqrt(jnp.sum(kp * kp, -1, keepdims=True) + EPS)
    vc_ref[...] = qkv[:, 2 * qw:]

    ab = ab_ref[...]
    g = -jnp.exp(al_ref[...]) * jax.nn.softplus(ab + dtb_ref[...])
    beta_ref[...] = jax.nn.sigmoid(ab)
    row = lax.broadcasted_iota(jnp.int32, (rb, LANES), 0) % c
    for k in range(int(math.log2(c))):
        g = g + jnp.where(row >= (1 << k), pltpu.roll(g, 1 << k, 0), 0.0)
    gc_ref[...] = g

    ri = lax.broadcasted_iota(jnp.int32, (c, c), 0)
    ci = lax.broadcasted_iota(jnp.int32, (c, c), 1)
    causal = ri >= ci
    strict = ri > ci
    nw = nw_ref[...]

    def prepare_body(it, carry):
        rows = [pl.multiple_of((it * PREP + j) * c, c) for j in range(PREP)]
        scores = []
        for j, r0 in enumerate(rows):
            gcc = gc_ref[pl.ds(r0, c), :]
            gl_ref[it * PREP + j] = jnp.broadcast_to(jnp.exp(gcc[c - 1:c, :]), (8, LANES))
            for p in range(nqk):
                qp = qn_ref[pl.ds(r0, c), p * hd:(p + 1) * hd]
                kp = kn_ref[pl.ds(r0, c), p * hd:(p + 1) * hd]
                qk_kk = _dot_nt(jnp.concatenate([qp, kp], axis=0).astype(BF16), kp.astype(BF16))
                scores.append((r0, p, gcc, qp, kp, qk_kk))
        lows, rhss, where = [], [], []
        for r0, p, gcc, qp, kp, qk_kk in scores:
            gct = gcc.T
            bet = beta_ref[pl.ds(r0, c), :]
            for h in (2 * p, 2 * p + 1):
                cols = slice(h * hd, (h + 1) * hd)
                gcol = jnp.broadcast_to(gcc[:, h:h + 1], (c, hd))
                bcol = jnp.broadcast_to(bet[:, nh + h:nh + h + 1], (c, hd))
                dec = jnp.exp(jnp.where(causal, gcol[:, :c] - gct[h:h + 1, :], -jnp.inf))
                lows.append(jnp.where(strict, bcol[:, :c] * qk_kk[c:] * dec, 0.0))
                at_ref[pl.ds(r0, c), h * hd:h * hd + c] = (qk_kk[:c] * dec).astype(BF16)
                eg = jnp.exp(gcol)
                qd_ref[pl.ds(r0, c), cols] = (qp * eg).astype(BF16)
                kd_ref[pl.ds(r0, c), cols] = (kp * jnp.exp(gcc[c - 1:c, h:h + 1] - gcol)).astype(BF16)
                rhss.append(jnp.concatenate([vc_ref[pl.ds(r0, c), cols] * bcol, kp * (bcol * eg)],
                                            axis=1).astype(BF16))
                where.append((r0, cols))
        t_groups = _unit_lower_inverse([lows[i:i + 4] for i in range(0, len(lows), 4)])
        t_mats = [t for group in t_groups for t in group]
        uws = [_dot(t.astype(BF16), rhs) for t, rhs in zip(t_mats, rhss)]
        for uw, (r0, cols) in zip(uws, where):
            uv_ref[pl.ds(r0, c), cols] = uw[:, :hd]
            wk_ref[pl.ds(r0, c), cols] = uw[:, hd:].astype(BF16)
        return carry

    lax.fori_loop(0, rb // (PREP * c), prepare_body, 0)

    def state_body(n, carry):
        r0 = pl.multiple_of(n * c, c)
        gl = gl_ref[n]
        heads = [slice(h * hd, (h + 1) * hd) for h in range(nh)]
        sts = [s_ref[h] for h in range(nh)]
        ws_qs = [_dot(jnp.concatenate([wk_ref[pl.ds(r0, c), cols], qd_ref[pl.ds(r0, c), cols]], axis=0),
                      st.astype(BF16)) for cols, st in zip(heads, sts)]
        v16s = [(uv_ref[pl.ds(r0, c), cols] - wq[:c]).astype(BF16) for cols, wq in zip(heads, ws_qs)]
        upd = [_dot_tn(kd_ref[pl.ds(r0, c), cols], v16) for cols, v16 in zip(heads, v16s)]
        intra = [_dot(at_ref[pl.ds(r0, c), h * hd:h * hd + c], v16) for h, v16 in enumerate(v16s)]
        for h, cols in enumerate(heads):
            s_ref[h] = sts[h] * gl[0:1, h:h + 1] + upd[h]
            o = ws_qs[h][c:] + intra[h]
            zh = z_ref[pl.ds(r0, c), cols].astype(F32)
            on = o * lax.rsqrt(jnp.mean(o * o, -1, keepdims=True) + EPS) * nw
            o_ref[pl.ds(r0, c), cols] = (on * _silu(zh)).astype(o_ref.dtype)
        return carry

    lax.fori_loop(0, rb // c, state_body, 0, unroll=True)


def _gdn_mixer(proj, ab, conv_w, a_log, dt_bias, norm_w, l, rb):
    hd, nh = GDN_HEAD_DIM, GDN_V_HEADS
    qw = (nh // 2) * hd
    vw = nh * hd
    pad = LANES - nh
    al = jnp.pad(a_log.astype(F32), (0, pad)).reshape(1, LANES)
    dtb = jnp.pad(dt_bias.astype(F32), (0, pad)).reshape(1, LANES)
    kern = functools.partial(_gdn_kernel, rb=rb)
    return pl.pallas_call(
        kern,
        grid=(l // rb,),
        in_specs=[pl.BlockSpec((rb, qw), lambda i: (i, 2)),
                  pl.BlockSpec((rb, qw), lambda i: (i, 3)),
                  pl.BlockSpec((rb, vw), lambda i: (i, 2)),
                  pl.BlockSpec((rb, vw), lambda i: (i, 3)),
                  pl.BlockSpec((rb, LANES), lambda i: (i, 0)),
                  pl.BlockSpec((GDN_CONV, 2 * qw + vw), lambda i: (0, 0)),
                  pl.BlockSpec((1, LANES), lambda i: (0, 0)),
                  pl.BlockSpec((1, LANES), lambda i: (0, 0)),
                  pl.BlockSpec((1, hd), lambda i: (0, 0))],
        out_specs=pl.BlockSpec((rb, vw), lambda i: (i, 0)),
        out_shape=jax.ShapeDtypeStruct((l, vw), BF16),
        scratch_shapes=[pltpu.VMEM((rb + 8, 2 * qw + vw), F32),
                        pltpu.VMEM((rb, qw), F32),
                        pltpu.VMEM((rb, qw), F32),
                        pltpu.VMEM((rb, vw), F32),
                        pltpu.VMEM((rb, LANES), F32),
                        pltpu.VMEM((rb, LANES), F32),
                        pltpu.VMEM((nh, hd, hd), F32),
                        pltpu.VMEM((rb, vw), F32),
                        pltpu.VMEM((rb, vw), BF16),
                        pltpu.VMEM((rb, vw), BF16),
                        pltpu.VMEM((rb, vw), BF16),
                        pltpu.VMEM((rb, vw), BF16),
                        pltpu.VMEM((rb // CHUNK, 8, LANES), F32)],
        compiler_params=_params(("arbitrary",)),
        name="gdn_mixer",
    )(proj, proj, proj, proj, ab, conv_w.astype(F32), al, dtb, norm_w.astype(F32).reshape(1, hd))


def _gla_kernel(q_ref, k_ref, v_ref, r_ref, gl_ref, w2_ref, gb_ref, nw_ref, o_ref,
                qt_ref, kt_ref, gle_ref, st_ref, *, rb):
    nh = GLA_HEADS
    dk = q_ref.shape[1] // nh
    dv = v_ref.shape[1] // nh
    c = CHUNK

    @pl.when(pl.program_id(0) == 0)
    def _():
        st_ref[...] = jnp.zeros_like(st_ref)

    x = _dot_split(gl_ref[...], w2_ref[...]) + gb_ref[...]
    b = jax.nn.log_sigmoid(x) / GLA_TAU
    row = lax.broadcasted_iota(jnp.int32, b.shape, 0) % c
    for k in range(int(math.log2(c))):
        b = b + jnp.where(row >= (1 << k), pltpu.roll(b, 1 << k, 0), 0.0)
    q = q_ref[...].astype(F32) * (dk ** -0.5)
    kf = k_ref[...].astype(F32)
    qt_ref[...] = (q * jnp.exp(b)).astype(BF16)
    kt_ref[...] = (kf * jnp.exp(-b)).astype(BF16)
    for n in range(rb // c):
        gle_ref[n] = jnp.broadcast_to(jnp.exp(b[(n + 1) * c - 1:(n + 1) * c, :]), (8, b.shape[1]))

    ri = lax.broadcasted_iota(jnp.int32, (c, c), 0)
    ci = lax.broadcasted_iota(jnp.int32, (c, c), 1)
    causal = ri >= ci
    nw = nw_ref[...]

    def chunk_body(it, carry):
        sts = [st_ref[h] for h in range(nh)]
        work = []
        for j in range(GLA_GROUP):
            n = it * GLA_GROUP + j
            r0 = pl.multiple_of(n * c, c)
            qts = [qt_ref[pl.ds(r0, c), h * dk:(h + 1) * dk] for h in range(nh)]
            kts = [kt_ref[pl.ds(r0, c), h * dk:(h + 1) * dk] for h in range(nh)]
            vhs = [v_ref[pl.ds(r0, c), h * dv:(h + 1) * dv] for h in range(nh)]
            scores = [_dot_nt(qts[h], kts[h]) for h in range(nh)]
            upd = [_dot_tn(vhs[h], kts[h]) for h in range(nh)]
            work.append((n, r0, qts, vhs, scores, upd))
        for n, r0, qts, vhs, scores, upd in work:
            gle = gle_ref[n]
            inter = [_dot_nt(qts[h], sts[h].astype(BF16)) for h in range(nh)]
            intra = [_dot(jnp.where(causal, scores[h], 0.0).astype(BF16), vhs[h]) for h in range(nh)]
            sts = [(sts[h] + upd[h]) * gle[0:1, h * dk:(h + 1) * dk] for h in range(nh)]
            for h in range(nh):
                o = intra[h] + inter[h]
                rh = r_ref[pl.ds(r0, c), h * dv:(h + 1) * dv].astype(F32)
                on = o * lax.rsqrt(jnp.mean(o * o, -1, keepdims=True) + EPS) * nw
                o_ref[pl.ds(r0, c), h * dv:(h + 1) * dv] = (on * _silu(rh)).astype(o_ref.dtype)
        for h in range(nh):
            st_ref[h] = sts[h]
        return carry

    lax.fori_loop(0, rb // (GLA_GROUP * c), chunk_body, 0, unroll=True)


def _gla_mixer(proj, g_low, gate_w2, gate_b, norm_w, l, rb, dk_all, dv_all):
    nh = GLA_HEADS
    lowrank = gate_w2.shape[0]
    w2 = jnp.pad(gate_w2.astype(F32), ((0, LANES - lowrank), (0, 0)))
    kern = functools.partial(_gla_kernel, rb=rb)
    return pl.pallas_call(
        kern,
        grid=(l // rb,),
        in_specs=[pl.BlockSpec((rb, dk_all), lambda i: (i, 0)),
                  pl.BlockSpec((rb, dk_all), lambda i: (i, 1)),
                  pl.BlockSpec((rb, dv_all), lambda i: (i, 1)),
                  pl.BlockSpec((rb, dv_all), lambda i: (i, 2)),
                  pl.BlockSpec((rb, LANES), lambda i: (i, 0)),
                  pl.BlockSpec((LANES, dk_all), lambda i: (0, 0)),
                  pl.BlockSpec((1, dk_all), lambda i: (0, 0)),
                  pl.BlockSpec((1, dv_all // nh), lambda i: (0, 0))],
        out_specs=pl.BlockSpec((rb, dv_all), lambda i: (i, 0)),
        out_shape=jax.ShapeDtypeStruct((l, dv_all), BF16),
        scratch_shapes=[pltpu.VMEM((rb, dk_all), BF16),
                        pltpu.VMEM((rb, dk_all), BF16),
                        pltpu.VMEM((rb // CHUNK, 8, dk_all), F32),
                        pltpu.VMEM((nh, dv_all // nh, dk_all // nh), F32)],
        compiler_params=_params(("arbitrary",)),
        name="gla_mixer",
    )(proj, proj, proj, proj, g_low, w2, gate_b.astype(F32).reshape(1, dk_all),
      norm_w.astype(F32).reshape(1, dv_all // nh))


def _out0_kernel(x_ref, ya_ref, yb_ref, gw_ref, gb_ref, wa_ref, wb_ref, pw_ref, gt_ref, o_ref):
    ya = ya_ref[...]
    gate = jax.nn.sigmoid(_dot(ya, gw_ref[...]) + gb_ref[...])
    ya = (ya.astype(F32) * gate).astype(BF16)
    m = _dot(ya, wa_ref[...]) + _dot(yb_ref[...], wb_ref[...])
    o_ref[...] = _postnorm_residual(x_ref[...], m, pw_ref[...], gt_ref[...])


def _out1_kernel(x_ref, y_ref, w_ref, pw_ref, gt_ref, o_ref):
    m = _dot(y_ref[...], w_ref[...])
    o_ref[...] = _postnorm_residual(x_ref[...], m, pw_ref[...], gt_ref[...])


def _const_spec(shape):
    return pl.BlockSpec(shape, lambda i: (0,) * len(shape), pipeline_mode=pl.Buffered(1))


def _out_project0(x, ya, yb, glu_w, glu_b, w_out, post_w, mod, tm):
    l, d = x.shape
    wa = ya.shape[1]
    wb = yb.shape[1]
    return pl.pallas_call(
        _out0_kernel,
        grid=(l // tm,),
        in_specs=[pl.BlockSpec((tm, d), lambda i: (i, 0)),
                  pl.BlockSpec((tm, wa), lambda i: (i, 0)),
                  pl.BlockSpec((tm, wb), lambda i: (i, 0)),
                  _const_spec((wa, wa)),
                  _const_spec((1, wa)),
                  pl.BlockSpec((wa, d), lambda i: (0, 0), pipeline_mode=pl.Buffered(1)),
                  pl.BlockSpec((wb, d), lambda i: (1, 0), pipeline_mode=pl.Buffered(1)),
                  _const_spec((1, d)),
                  pl.BlockSpec((1, d), lambda i: (0, 2))],
        out_specs=pl.BlockSpec((tm, d), lambda i: (i, 0)),
        out_shape=jax.ShapeDtypeStruct((l, d), F32),
        compiler_params=_params(("arbitrary",)),
        name="out_project0",
    )(x, ya, yb, glu_w, glu_b.astype(F32).reshape(1, wa), w_out, w_out, post_w.reshape(1, d), mod)


def _out_project1(x, y, w_out, post_w, mod, tm):
    l, d = x.shape
    w = y.shape[1]
    return pl.pallas_call(
        _out1_kernel,
        grid=(l // tm,),
        in_specs=[pl.BlockSpec((tm, d), lambda i: (i, 0)),
                  pl.BlockSpec((tm, w), lambda i: (i, 0)),
                  _const_spec((w, d)),
                  _const_spec((1, d)),
                  pl.BlockSpec((1, d), lambda i: (0, 2))],
        out_specs=pl.BlockSpec((tm, d), lambda i: (i, 0)),
        out_shape=jax.ShapeDtypeStruct((l, d), F32),
        compiler_params=_params(("arbitrary",)),
        name="out_project1",
    )(x, y, w_out, post_w.reshape(1, d), mod)


def _ffn_kernel(x_ref, nw_ref, sc_ref, sh_ref, wg_ref, wu_ref, wd_ref, pw_ref, gt_ref, o_ref,
                h_ref, a_ref, *, n_hid, n_out):
    j = pl.program_id(1)
    th = a_ref.shape[2]
    tn = wd_ref.shape[1]

    @pl.when(j == 0)
    def _():
        h_ref[...] = _prenorm(x_ref[...], nw_ref[...], sc_ref[...], sh_ref[...]).astype(BF16)

    @pl.when(j < n_hid)
    def _():
        h = h_ref[...]
        a_ref[j] = (_silu(_dot(h, wg_ref[...])) * _dot(h, wu_ref[...])).astype(BF16)

    for jj in range(n_out):
        @pl.when(j == n_hid + jj)
        def _():
            acc = _dot(a_ref[0], wd_ref[0:th, :])
            for k in range(1, n_hid):
                acc = acc + _dot(a_ref[k], wd_ref[k * th:(k + 1) * th, :])
            o_ref[:, jj * tn:(jj + 1) * tn] = acc

    @pl.when(j == n_hid + n_out - 1)
    def _():
        o_ref[...] = _postnorm_residual(x_ref[...], o_ref[...], pw_ref[...], gt_ref[...])


def _ffn_block(x, mod, pre_w, post_w, w_gate, w_up, w_down, tm, th, tn):
    l, d = x.shape
    hid = w_gate.shape[1]
    n_hid, n_out = hid // th, d // tn
    kern = functools.partial(_ffn_kernel, n_hid=n_hid, n_out=n_out)
    return pl.pallas_call(
        kern,
        grid=(l // tm, n_hid + n_out),
        in_specs=[pl.BlockSpec((tm, d), lambda i, j: (i, 0)),
                  pl.BlockSpec((1, d), lambda i, j: (0, 0)),
                  pl.BlockSpec((1, d), lambda i, j: (0, 4)),
                  pl.BlockSpec((1, d), lambda i, j: (0, 3)),
                  pl.BlockSpec((d, th), lambda i, j: (0, jnp.minimum(j, n_hid - 1))),
                  pl.BlockSpec((d, th), lambda i, j: (0, jnp.minimum(j, n_hid - 1))),
                  pl.BlockSpec((hid, tn), lambda i, j: (0, jnp.maximum(j - n_hid, 0))),
                  pl.BlockSpec((1, d), lambda i, j: (0, 0)),
                  pl.BlockSpec((1, d), lambda i, j: (0, 5))],
        out_specs=pl.BlockSpec((tm, d), lambda i, j: (i, 0)),
        out_shape=jax.ShapeDtypeStruct((l, d), F32),
        scratch_shapes=[pltpu.VMEM((tm, d), BF16),
                        pltpu.VMEM((n_hid, tm, th), BF16)],
        compiler_params=_params(("arbitrary", "arbitrary")),
        name="ffn_block",
    )(x, pre_w.reshape(1, d), mod, mod, w_gate, w_up, w_down, post_w.reshape(1, d), mod)


class _Tiles(NamedTuple):
    tm: int
    tp: int
    tc: int
    tf: int
    th: int
    tn: int
    rb: int
    cb: int


def _tiles(l):
    return _Tiles(tm=min(512, l), tp=min(1024, l), tc=1024, tf=min(512, l), th=512, tn=512,
                  rb=min(512, l), cb=min(512, l // S5_T))


def kernel(x, c, ada_w0, ada_b0, mix_pre0, mix_post0, ffn_pre0, ffn_post0, w_in0, s5_lambda_re, s5_lambda_im, s5_log_step, s5_b_re, s5_b_im, s5_c_re, s5_c_im, s5_d, s5_glu_w, s5_glu_b, gdn_conv_w, gdn_a_log, gdn_dt_bias, gdn_norm_w, w_out0, ffn_gate0, ffn_up0, ffn_down0, ada_w1, ada_b1, mix_pre1, mix_post1, ffn_pre1, ffn_post1, w_in1, gla_gate_w2, gla_gate_b, gla_norm_w, w_out1, ffn_gate1, ffn_up1, ffn_down1):
    bsz, l, d = x.shape
    assert bsz == 1
    x = x.reshape(l, d)
    t = _tiles(l)
    tm, tp, tc, tf, th, tn, rb, cb = t.tm, t.tp, t.tc, t.tf, t.th, t.tn, t.rb, t.cb
    n_hs = int(math.log2(cb))
    assert l % tm == 0 and l % tp == 0 and l % rb == 0 and l % (cb * S5_T) == 0 and (1 << n_hs) == cb

    s5_w = s5_glu_w.shape[0]
    qk_w = (GDN_V_HEADS // 2) * GDN_HEAD_DIM
    v_w = GDN_V_HEADS * GDN_HEAD_DIM
    n0 = s5_w + 2 * qk_w + 2 * v_w
    dk_all = gla_gate_w2.shape[1]
    dv_all = w_out1.shape[0]
    n1 = 2 * dk_all + 2 * dv_all

    mod0 = _ada_modulation(c, ada_w0, ada_b0)
    mod1 = _ada_modulation(c, ada_w1, ada_b1)
    proj0, ab = _prenorm_project(x, mod0, 0, mix_pre0, w_in0, n0, tp, tc)
    y_a = _s5_mixer(proj0, s5_lambda_re, s5_lambda_im, s5_log_step, s5_b_re, s5_b_im, s5_c_re, s5_c_im, s5_d,
                    l, cb, n_hs)
    y_b = _gdn_mixer(proj0, ab, gdn_conv_w, gdn_a_log, gdn_dt_bias, gdn_norm_w, l, rb)
    x = _out_project0(x, y_a, y_b, s5_glu_w.astype(BF16), s5_glu_b, w_out0.astype(BF16), mix_post0, mod0, tm)
    x = _ffn_block(x, mod0, ffn_pre0, ffn_post0, ffn_gate0.astype(BF16), ffn_up0.astype(BF16),
                   ffn_down0.astype(BF16), tf, th, tn)

    proj1, g_low = _prenorm_project(x, mod1, 0, mix_pre1, w_in1, n1, tp, tc)
    y_c = _gla_mixer(proj1, g_low, gla_gate_w2, gla_gate_b, gla_norm_w, l, rb, dk_all, dv_all)
    x = _out_project1(x, y_c, w_out1.astype(BF16), mix_post1, mod1, tm)
    x = _ffn_block(x, mod1, ffn_pre1, ffn_post1, ffn_gate1.astype(BF16), ffn_up1.astype(BF16),
                   ffn_down1.astype(BF16), tf, th, tn)
    return x.reshape(bsz, l, d)
```

```python
import functools
import math
from typing import NamedTuple

import jax
import jax.numpy as jnp
from jax import lax
from jax.experimental import pallas as pl
from jax.experimental.pallas import tpu as pltpu

F32 = jnp.float32
BF16 = jnp.bfloat16
EPS = 1e-6

LANES = 128
MXU_N = 256
VMEM_LIMIT = 56 << 20

S5_GROUP = 16
S5_STATE = 64
S5_T = 16
GDN_HEAD_DIM = 128
GDN_V_HEADS = 8
GDN_CONV = 4
CHUNK = 64
INV_BASE = 16
PREP = 8
GLA_HEADS = 4
GLA_TAU = 16.0
GLA_GROUP = 2


def _params(sem):
    return pltpu.CompilerParams(dimension_semantics=sem, vmem_limit_bytes=VMEM_LIMIT)


def _dot(a, b):
    return jnp.dot(a, b, preferred_element_type=F32)


def _dot_nt(a, b):
    return lax.dot_general(a, b, (((1,), (1,)), ((), ())), preferred_element_type=F32)


def _dot_tn(a, b):
    return lax.dot_general(a, b, (((0,), (0,)), ((), ())), preferred_element_type=F32)


def _split_bf16(x):
    hi = x.astype(BF16)
    lo = (x - hi.astype(F32)).astype(BF16)
    return hi, lo


def _dot_split(a, b):
    ah, al = _split_bf16(a)
    bh, bl = _split_bf16(b)
    return _dot(ah, bh) + (_dot(ah, bl) + _dot(al, bh))


def _silu(x):
    return x * jax.nn.sigmoid(x)


def _prenorm(x, nw, sc, sh):
    ms = jnp.mean(x * x, axis=-1, keepdims=True)
    return (x * lax.rsqrt(ms + EPS)) * (nw * (1.0 + sc)) + sh


def _postnorm_residual(x, m, pw, gt):
    ms = jnp.mean(m * m, axis=-1, keepdims=True)
    return x + (m * lax.rsqrt(ms + EPS)) * (gt * pw)


def _shift_rows(x, s, row):
    return jnp.where(row >= s, pltpu.roll(x, s, 0), 0.0)


def _ada_kernel(c_ref, w_ref, b_ref, o_ref):
    s = _silu(c_ref[...])
    o_ref[...] = jnp.sum(s * w_ref[...], axis=0, keepdims=True) + b_ref[...]


def _ada_modulation(c, w, b):
    d, n = w.shape
    tn = 1024
    return pl.pallas_call(
        _ada_kernel,
        grid=(n // tn,),
        in_specs=[pl.BlockSpec((d, 1), lambda j: (0, 0)),
                  pl.BlockSpec((d, tn), lambda j: (0, j)),
                  pl.BlockSpec((1, tn), lambda j: (0, j))],
        out_specs=pl.BlockSpec((1, tn), lambda j: (0, j)),
        out_shape=jax.ShapeDtypeStruct((1, n), F32),
        compiler_params=_params(("arbitrary",)),
        name="ada_modulation",
    )(c.reshape(d, 1), w, b.reshape(1, n))


def _proj_kernel(x_ref, nw_ref, sc_ref, sh_ref, w_ref, ws_ref, o_ref, os_ref, h_ref):
    @pl.when(pl.program_id(1) == 0)
    def _():
        hb = _prenorm(x_ref[...], nw_ref[...], sc_ref[...], sh_ref[...]).astype(BF16)
        h_ref[...] = hb
        os_ref[...] = _dot(hb, ws_ref[...])

    o_ref[...] = _dot(h_ref[...], w_ref[...]).astype(o_ref.dtype)


def _prenorm_project(x, mod, mod_base, norm_w, w_all, n, tm, tn):
    l, d = x.shape
    w_main = w_all.astype(BF16)
    w_small = jnp.pad(w_all[:, n:], ((0, 0), (0, LANES - (w_all.shape[1] - n)))).astype(BF16)
    ns = LANES
    return pl.pallas_call(
        _proj_kernel,
        grid=(l // tm, n // tn),
        in_specs=[pl.BlockSpec((tm, d), lambda i, j: (i, 0)),
                  pl.BlockSpec((1, d), lambda i, j: (0, 0)),
                  pl.BlockSpec((1, d), lambda i, j: (0, mod_base + 1)),
                  pl.BlockSpec((1, d), lambda i, j: (0, mod_base)),
                  pl.BlockSpec((d, tn), lambda i, j: (0, j)),
                  pl.BlockSpec((d, ns), lambda i, j: (0, 0))],
        out_specs=[pl.BlockSpec((tm, tn), lambda i, j: (i, j)),
                   pl.BlockSpec((tm, ns), lambda i, j: (i, 0))],
        out_shape=[jax.ShapeDtypeStruct((l, n), BF16), jax.ShapeDtypeStruct((l, ns), F32)],
        scratch_shapes=[pltpu.VMEM((tm, d), BF16)],
        compiler_params=_params(("arbitrary", "arbitrary")),
        name="prenorm_project",
    )(x, norm_w.reshape(1, d), mod, mod, w_main, w_small)


def _s5_build_tables(b_ref, c_ref, lam_ref, d_ref, wi_ref, wb_ref, wc_ref, hs_ref, n_hs):
    t = S5_T
    half = b_ref.shape[2] // 2
    lr, li = lam_ref[0, 0:1, :], lam_ref[0, 1:2, :]
    dt = jnp.exp(lam_ref[0, 2:3, :])
    mag = jnp.exp(lr * dt)
    ar, ai = mag * jnp.cos(li * dt), mag * jnp.sin(li * dt)
    den = lr * lr + li * li
    nr, ni = ar - 1.0, ai
    f_re = (nr * lr + ni * li) / den
    f_im = (ni * lr - nr * li) / den
    b_re, b_im = b_ref[0, :, :half], b_ref[0, :, half:]
    bb_re = f_re * b_re - f_im * b_im
    bb_im = f_re * b_im + f_im * b_re
    c_re, c_im = c_ref[0, :, :half], c_ref[0, :, half:]
    bb_hi, bb_lo = _split_bf16(jnp.concatenate([bb_re, bb_im], axis=1))
    rr = lax.broadcasted_iota(jnp.int32, (LANES, LANES), 0)
    qq = lax.broadcasted_iota(jnp.int32, (LANES, LANES), 1)
    skip = jnp.where(rr == qq, d_ref[0], 0.0)

    pr, pi = jnp.ones_like(ar), jnp.zeros_like(ar)
    pows = []
    for d in range(t + 1):
        pows.append((pr, pi))
        cs = jnp.concatenate([c_re * pr - c_im * pi, -(c_re * pi + c_im * pr)], axis=1)
        if d < t:
            cs_hi, cs_lo = _split_bf16(cs)
            k_d = _dot_nt(bb_hi, cs_hi) + (_dot_nt(bb_hi, cs_lo) + _dot_nt(bb_lo, cs_hi))
            if d == 0:
                k_d = k_d + skip
            k_d = k_d.astype(BF16)
            for s in range(t - d):
                wi_ref[s * LANES:(s + 1) * LANES, (s + d) * LANES:(s + d + 1) * LANES] = k_d
        if d >= 1:
            wc_ref[:, (d - 1) * LANES:d * LANES] = cs.T.astype(BF16)
        pr, pi = pr * ar - pi * ai, pr * ai + pi * ar
    for tb in range(t * LANES // MXU_N):
        wi_ref[(2 * tb + 1) * LANES:(2 * tb + 2) * LANES, 2 * tb * LANES:(2 * tb + 1) * LANES] = (
            jnp.zeros((LANES, LANES), BF16))
    for s in range(t):
        pr, pi = pows[t - 1 - s]
        wb_ref[s * LANES:(s + 1) * LANES, :half] = (bb_re * pr - bb_im * pi).astype(BF16)
        wb_ref[s * LANES:(s + 1) * LANES, half:] = (bb_re * pi + bb_im * pr).astype(BF16)
    pr, pi = pows[t]
    for k in range(n_hs):
        hs_ref[k:k + 1, :half] = pr
        hs_ref[k:k + 1, half:] = pi
        pr, pi = pr * pr - pi * pi, 2.0 * pr * pi


def _s5_kernel(u_ref, b_ref, c_ref, lam_ref, d_ref, o_ref,
               wi_ref, wb_ref, wc_ref, hs_ref, uf_ref, u2_ref, yf_ref, carry_ref, *, cb, n_hs):
    t = S5_T
    half = carry_ref.shape[1] // 2

    @pl.when(pl.program_id(1) == 0)
    def _():
        carry_ref[...] = jnp.zeros_like(carry_ref)
        _s5_build_tables(b_ref, c_ref, lam_ref, d_ref, wi_ref, wb_ref, wc_ref, hs_ref, n_hs)

    uf_ref[...] = u_ref[...].astype(F32)
    for k in range(t):
        u2_ref[:, k * LANES:(k + 1) * LANES] = uf_ref[pl.ds(k, cb, stride=t), :].astype(BF16)
    u2 = u2_ref[...]

    s = _dot(u2, wb_ref[...])
    n_tb = t * LANES // MXU_N
    intra = [_dot(u2[:, :(tb + 1) * MXU_N], wi_ref[:(tb + 1) * MXU_N, tb * MXU_N:(tb + 1) * MXU_N])
             for tb in range(n_tb)]
    sr, si = s[:, :half], s[:, half:]
    row = lax.broadcasted_iota(jnp.int32, (cb, half), 0)
    cr, ci = carry_ref[:, :half], carry_ref[:, half:]
    pr, pi = hs_ref[0:1, :half], hs_ref[0:1, half:]
    sr = sr + jnp.where(row == 0, pr * cr - pi * ci, 0.0)
    si = si + jnp.where(row == 0, pr * ci + pi * cr, 0.0)
    for k in range(n_hs):
        pr, pi = hs_ref[k:k + 1, :half], hs_ref[k:k + 1, half:]
        shr, shi = _shift_rows(sr, 1 << k, row), _shift_rows(si, 1 << k, row)
        sr, si = sr + pr * shr - pi * shi, si + pr * shi + pi * shr
    epr = jnp.where(row >= 1, pltpu.roll(sr, 1, 0), cr)
    epi = jnp.where(row >= 1, pltpu.roll(si, 1, 0), ci)
    carry_ref[:, :half] = sr[cb - 1:cb]
    carry_ref[:, half:] = si[cb - 1:cb]
    ep = jnp.concatenate([epr, epi], axis=1).astype(BF16)
    y_inter = _dot(ep, wc_ref[...])

    for tb in range(n_tb):
        yb = jax.nn.gelu(y_inter[:, tb * MXU_N:(tb + 1) * MXU_N] + intra[tb], approximate=True)
        for q in range(MXU_N // LANES):
            yf_ref[pl.ds(tb * (MXU_N // LANES) + q, cb, stride=t), :] = yb[:, q * LANES:(q + 1) * LANES]
    o_ref[...] = yf_ref[...].astype(o_ref.dtype)


def _s5_group_rows(x, nl):
    g, p, n = x.shape
    gl = g // nl
    x = jnp.tile(x.astype(F32).reshape(nl, gl * p, n), (1, 1, gl))
    rg = jnp.arange(gl * p)[:, None] // p
    cg = jnp.arange(gl * n)[None, :] // n
    return jnp.where(rg == cg, x, 0.0)


def _s5_mixer(proj, lam_re, lam_im, log_step, b_re, b_im, c_re, c_im, d_skip, l, cb, n_hs):
    t = S5_T
    g, n = lam_re.shape
    nl = g * S5_GROUP // LANES
    half = (g // nl) * n
    rb = cb * t
    brow = jnp.concatenate([_s5_group_rows(b_re.transpose(0, 2, 1), nl),
                            _s5_group_rows(b_im.transpose(0, 2, 1), nl)], axis=-1)
    crow = jnp.concatenate([_s5_group_rows(c_re, nl), _s5_group_rows(c_im, nl)], axis=-1)
    lam = jnp.stack([lam_re.astype(F32).reshape(nl, half), lam_im.astype(F32).reshape(nl, half),
                     jnp.repeat(log_step.astype(F32), n).reshape(nl, half)], axis=1)
    dsk = d_skip.astype(F32).reshape(nl, 1, LANES)
    kern = functools.partial(_s5_kernel, cb=cb, n_hs=n_hs)
    return pl.pallas_call(
        kern,
        grid=(nl, l // rb),
        in_specs=[pl.BlockSpec((rb, LANES), lambda a, b: (b, a)),
                  pl.BlockSpec((1, LANES, 2 * half), lambda a, b: (a, 0, 0)),
                  pl.BlockSpec((1, LANES, 2 * half), lambda a, b: (a, 0, 0)),
                  pl.BlockSpec((1, 3, half), lambda a, b: (a, 0, 0)),
                  pl.BlockSpec((1, 1, LANES), lambda a, b: (a, 0, 0))],
        out_specs=pl.BlockSpec((rb, LANES), lambda a, b: (b, a)),
        out_shape=jax.ShapeDtypeStruct((l, nl * LANES), BF16),
        scratch_shapes=[pltpu.VMEM((t * LANES, t * LANES), BF16),
                        pltpu.VMEM((t * LANES, 2 * half), BF16),
                        pltpu.VMEM((2 * half, t * LANES), BF16),
                        pltpu.VMEM((n_hs, 2 * half), F32),
                        pltpu.VMEM((rb, LANES), F32),
                        pltpu.VMEM((cb, t * LANES), BF16),
                        pltpu.VMEM((rb, LANES), F32),
                        pltpu.VMEM((1, 2 * half), F32)],
        compiler_params=_params(("arbitrary", "arbitrary")),
        name="s5_mixer",
    )(proj, brow, crow, lam, dsk)


def _unit_lower_inverse(lows):
    c = lows[0][0].shape[0]
    nb = len(lows[0])
    r = lax.broadcasted_iota(jnp.int32, (c, nb * c), 0)
    q = lax.broadcasted_iota(jnp.int32, (c, nb * c), 1)
    eye_w = jnp.where(q % c == r, 1.0, 0.0)
    rr = lax.broadcasted_iota(jnp.int32, (nb * c, nb * c), 0)
    qq = lax.broadcasted_iota(jnp.int32, (nb * c, nb * c), 1)
    on_diag = (rr // c) == (qq // c)

    def block_diag(xw):
        return jnp.where(on_diag, jnp.concatenate([xw] * nb, axis=0), 0.0).astype(BF16)

    def mm(xw, yw):
        return _dot(xw.astype(BF16), block_diag(yw))

    qc = q % c
    base = INV_BASE
    lws = [jnp.concatenate(group, axis=1) for group in lows]
    diag = [jnp.where(r // base == qc // base, lw, 0.0) for lw in lws]
    accs = [eye_w - d for d in diag]
    xs = [mm(d, d) for d in diag]
    steps = int(math.log2(base)) - 1
    for k in range(steps):
        last = k == steps - 1
        lhs = [acc if last else jnp.concatenate([x, acc], axis=0) for x, acc in zip(xs, accs)]
        both = [mm(a, x) for a, x in zip(lhs, xs)]
        accs = [acc + (b if last else b[c:]) for acc, b in zip(accs, both)]
        xs = [b[:c] for b in both]
    size = base
    while size < c:
        below = (r // (2 * size) == qc // (2 * size)) & (r // size != qc // size)
        subs = [jnp.where(below, lw, 0.0) for lw in lws]
        ys = [mm(s, acc) for s, acc in zip(subs, accs)]
        accs = [acc - mm(acc, y) for acc, y in zip(accs, ys)]
        size *= 2
    return [[acc[:, i * c:(i + 1) * c] for i in range(nb)] for acc in accs]


def _gdn_kernel(q_ref, k_ref, v_ref, z_ref, ab_ref, cw_ref, al_ref, dtb_ref, nw_ref, o_ref,
                xe_ref, qn_ref, kn_ref, vc_ref, gc_ref, beta_ref, s_ref,
                uv_ref, wk_ref, qd_ref, kd_ref, at_ref, gl_ref, *, rb):
    hd = GDN_HEAD_DIM
    nh = GDN_V_HEADS
    nqk = nh // 2
    qw = nqk * hd
    c = CHUNK
    halo = 8

    @pl.when(pl.program_id(0) == 0)
    def _():
        s_ref[...] = jnp.zeros_like(s_ref)
        xe_ref[0:halo, :] = jnp.zeros((halo, xe_ref.shape[1]), F32)

    xe_ref[halo:, 0:qw] = q_ref[...].astype(F32)
    xe_ref[halo:, qw:2 * qw] = k_ref[...].astype(F32)
    xe_ref[halo:, 2 * qw:] = v_ref[...].astype(F32)
    acc = cw_ref[GDN_CONV - 1:GDN_CONV, :] * xe_ref[halo:, :]
    for s in range(1, GDN_CONV):
        acc = acc + cw_ref[GDN_CONV - 1 - s:GDN_CONV - s, :] * xe_ref[halo - s:halo - s + rb, :]
    xe_ref[0:halo, :] = xe_ref[rb:rb + halo, :]
    qkv = _silu(acc)
    for p in range(nqk):
        qp = qkv[:, p * hd:(p + 1) * hd]
        qn_ref[:, p * hd:(p + 1) * hd] = qp * lax.rsqrt(jnp.sum(qp * qp, -1, keepdims=True) + EPS) * (hd ** -0.5)
        kp = qkv[:, qw + p * hd:qw + (p + 1) * hd]
        kn_ref[:, p * hd:(p + 1) * hd] = kp * lax.rsqrt(jnp.sum(kp * kp, -1, keepdims=True) + EPS)
    vc_ref[...] = qkv[:, 2 * qw:]

    ab = ab_ref[...]
    g = -jnp.exp(al_ref[...]) * jax.nn.softplus(ab + dtb_ref[...])
    beta_ref[...] = jax.nn.sigmoid(ab)
    row = lax.broadcasted_iota(jnp.int32, (rb, LANES), 0) % c
    for k in range(int(math.log2(c))):
        g = g + jnp.where(row >= (1 << k), pltpu.roll(g, 1 << k, 0), 0.0)
    gc_ref[...] = g

    ri = lax.broadcasted_iota(jnp.int32, (c, c), 0)
    ci = lax.broadcasted_iota(jnp.int32, (c, c), 1)
    causal = ri >= ci
    strict = ri > ci
    nw = nw_ref[...]

    def prepare_body(it, carry):
        rows = [pl.multiple_of((it * PREP + j) * c, c) for j in range(PREP)]
        scores = []
        for j, r0 in enumerate(rows):
            gcc = gc_ref[pl.ds(r0, c), :]
            gl_ref[it * PREP + j] = jnp.broadcast_to(jnp.exp(gcc[c - 1:c, :]), (8, LANES))
            for p in range(nqk):
                qp = qn_ref[pl.ds(r0, c), p * hd:(p + 1) * hd]
                kp = kn_ref[pl.ds(r0, c), p * hd:(p + 1) * hd]
                qk_kk = _dot_nt(jnp.concatenate([qp, kp], axis=0).astype(BF16), kp.astype(BF16))
                scores.append((r0, p, gcc, qp, kp, qk_kk))
        lows, rhss, where = [], [], []
        for r0, p, gcc, qp, kp, qk_kk in scores:
            gct = gcc.T
            bet = beta_ref[pl.ds(r0, c), :]
            for h in (2 * p, 2 * p + 1):
                cols = slice(h * hd, (h + 1) * hd)
                gcol = jnp.broadcast_to(gcc[:, h:h + 1], (c, hd))
                bcol = jnp.broadcast_to(bet[:, nh + h:nh + h + 1], (c, hd))
                dec = jnp.exp(jnp.where(causal, gcol[:, :c] - gct[h:h + 1, :], -jnp.inf))
                lows.append(jnp.where(strict, bcol[:, :c] * qk_kk[c:] * dec, 0.0))
                at_ref[pl.ds(r0, c), h * hd:h * hd + c] = (qk_kk[:c] * dec).astype(BF16)
                eg = jnp.exp(gcol)
                qd_ref[pl.ds(r0, c), cols] = (qp * eg).astype(BF16)
                kd_ref[pl.ds(r0, c), cols] = (kp * jnp.exp(gcc[c - 1:c, h:h + 1] - gcol)).astype(BF16)
                rhss.append(jnp.concatenate([vc_ref[pl.ds(r0, c), cols] * bcol, kp * (bcol * eg)],
                                            axis=1).astype(BF16))
                where.append((r0, cols))
        t_groups = _unit_lower_inverse([lows[i:i + 4] for i in range(0, len(lows), 4)])
        t_mats = [t for group in t_groups for t in group]
        uws = [_dot(t.astype(BF16), rhs) for t, rhs in zip(t_mats, rhss)]
        for uw, (r0, cols) in zip(uws, where):
            uv_ref[pl.ds(r0, c), cols] = uw[:, :hd]
            wk_ref[pl.ds(r0, c), cols] = uw[:, hd:].astype(BF16)
        return carry

    lax.fori_loop(0, rb // (PREP * c), prepare_body, 0)

    def state_body(n, carry):
        r0 = pl.multiple_of(n * c, c)
        gl = gl_ref[n]
        heads = [slice(h * hd, (h + 1) * hd) for h in range(nh)]
        sts = [s_ref[h] for h in range(nh)]
        ws_qs = [_dot(jnp.concatenate([wk_ref[pl.ds(r0, c), cols], qd_ref[pl.ds(r0, c), cols]], axis=0),
                      st.astype(BF16)) for cols, st in zip(heads, sts)]
        v16s = [(uv_ref[pl.ds(r0, c), cols] - wq[:c]).astype(BF16) for cols, wq in zip(heads, ws_qs)]
        upd = [_dot_tn(kd_ref[pl.ds(r0, c), cols], v16) for cols, v16 in zip(heads, v16s)]
        intra = [_dot(at_ref[pl.ds(r0, c), h * hd:h * hd + c], v16) for h, v16 in enumerate(v16s)]
        for h, cols in enumerate(heads):
            s_ref[h] = sts[h] * gl[0:1, h:h + 1] + upd[h]
            o = ws_qs[h][c:] + intra[h]
            zh = z_ref[pl.ds(r0, c), cols].astype(F32)
            on = o * lax.rsqrt(jnp.mean(o * o, -1, keepdims=True) + EPS) * nw
            o_ref[pl.ds(r0, c), cols] = (on * _silu(zh)).astype(o_ref.dtype)
        return carry

    lax.fori_loop(0, rb // c, state_body, 0, unroll=True)


def _gdn_mixer(proj, ab, conv_w, a_log, dt_bias, norm_w, l, rb):
    hd, nh = GDN_HEAD_DIM, GDN_V_HEADS
    qw = (nh // 2) * hd
    vw = nh * hd
    pad = LANES - nh
    al = jnp.pad(a_log.astype(F32), (0, pad)).reshape(1, LANES)
    dtb = jnp.pad(dt_bias.astype(F32), (0, pad)).reshape(1, LANES)
    kern = functools.partial(_gdn_kernel, rb=rb)
    return pl.pallas_call(
        kern,
        grid=(l // rb,),
        in_specs=[pl.BlockSpec((rb, qw), lambda i: (i, 2)),
                  pl.BlockSpec((rb, qw), lambda i: (i, 3)),
                  pl.BlockSpec((rb, vw), lambda i: (i, 2)),
                  pl.BlockSpec((rb, vw), lambda i: (i, 3)),
                  pl.BlockSpec((rb, LANES), lambda i: (i, 0)),
                  pl.BlockSpec((GDN_CONV, 2 * qw + vw), lambda i: (0, 0)),
                  pl.BlockSpec((1, LANES), lambda i: (0, 0)),
                  pl.BlockSpec((1, LANES), lambda i: (0, 0)),
                  pl.BlockSpec((1, hd), lambda i: (0, 0))],
        out_specs=pl.BlockSpec((rb, vw), lambda i: (i, 0)),
        out_shape=jax.ShapeDtypeStruct((l, vw), BF16),
        scratch_shapes=[pltpu.VMEM((rb + 8, 2 * qw + vw), F32),
                        pltpu.VMEM((rb, qw), F32),
                        pltpu.VMEM((rb, qw), F32),
                        pltpu.VMEM((rb, vw), F32),
                        pltpu.VMEM((rb, LANES), F32),
                        pltpu.VMEM((rb, LANES), F32),
                        pltpu.VMEM((nh, hd, hd), F32),
                        pltpu.VMEM((rb, vw), F32),
                        pltpu.VMEM((rb, vw), BF16),
                        pltpu.VMEM((rb, vw), BF16),
                        pltpu.VMEM((rb, vw), BF16),
                        pltpu.VMEM((rb, vw), BF16),
                        pltpu.VMEM((rb // CHUNK, 8, LANES), F32)],
        compiler_params=_params(("arbitrary",)),
        name="gdn_mixer",
    )(proj, proj, proj, proj, ab, conv_w.astype(F32), al, dtb, norm_w.astype(F32).reshape(1, hd))


def _gla_kernel(q_ref, k_ref, v_ref, r_ref, gl_ref, w2_ref, gb_ref, nw_ref, o_ref,
                qt_ref, kt_ref, gle_ref, st_ref, *, rb):
    nh = GLA_HEADS
    dk = q_ref.shape[1] // nh
    dv = v_ref.shape[1] // nh
    c = CHUNK

    @pl.when(pl.program_id(0) == 0)
    def _():
        st_ref[...] = jnp.zeros_like(st_ref)

    x = _dot_split(gl_ref[...], w2_ref[...]) + gb_ref[...]
    b = jax.nn.log_sigmoid(x) / GLA_TAU
    row = lax.broadcasted_iota(jnp.int32, b.shape, 0) % c
    for k in range(int(math.log2(c))):
        b = b + jnp.where(row >= (1 << k), pltpu.roll(b, 1 << k, 0), 0.0)
    q = q_ref[...].astype(F32) * (dk ** -0.5)
    kf = k_ref[...].astype(F32)
    qt_ref[...] = (q * jnp.exp(b)).astype(BF16)
    kt_ref[...] = (kf * jnp.exp(-b)).astype(BF16)
    for n in range(rb // c):
        gle_ref[n] = jnp.broadcast_to(jnp.exp(b[(n + 1) * c - 1:(n + 1) * c, :]), (8, b.shape[1]))

    ri = lax.broadcasted_iota(jnp.int32, (c, c), 0)
    ci = lax.broadcasted_iota(jnp.int32, (c, c), 1)
    causal = ri >= ci
    nw = nw_ref[...]

    def chunk_body(it, carry):
        sts = [st_ref[h] for h in range(nh)]
        work = []
        for j in range(GLA_GROUP):
            n = it * GLA_GROUP + j
            r0 = pl.multiple_of(n * c, c)
            qts = [qt_ref[pl.ds(r0, c), h * dk:(h + 1) * dk] for h in range(nh)]
            kts = [kt_ref[pl.ds(r0, c), h * dk:(h + 1) * dk] for h in range(nh)]
            vhs = [v_ref[pl.ds(r0, c), h * dv:(h + 1) * dv] for h in range(nh)]
            scores = [_dot_nt(qts[h], kts[h]) for h in range(nh)]
            upd = [_dot_tn(vhs[h], kts[h]) for h in range(nh)]
            work.append((n, r0, qts, vhs, scores, upd))
        for n, r0, qts, vhs, scores, upd in work:
            gle = gle_ref[n]
            inter = [_dot_nt(qts[h], sts[h].astype(BF16)) for h in range(nh)]
            intra = [_dot(jnp.where(causal, scores[h], 0.0).astype(BF16), vhs[h]) for h in range(nh)]
            sts = [(sts[h] + upd[h]) * gle[0:1, h * dk:(h + 1) * dk] for h in range(nh)]
            for h in range(nh):
                o = intra[h] + inter[h]
                rh = r_ref[pl.ds(r0, c), h * dv:(h + 1) * dv].astype(F32)
                on = o * lax.rsqrt(jnp.mean(o * o, -1, keepdims=True) + EPS) * nw
                o_ref[pl.ds(r0, c), h * dv:(h + 1) * dv] = (on * _silu(rh)).astype(o_ref.dtype)
        for h in range(nh):
            st_ref[h] = sts[h]
        return carry

    lax.fori_loop(0, rb // (GLA_GROUP * c), chunk_body, 0, unroll=True)


def _gla_mixer(proj, g_low, gate_w2, gate_b, norm_w, l, rb, dk_all, dv_all):
    nh = GLA_HEADS
    lowrank = gate_w2.shape[0]
    w2 = jnp.pad(gate_w2.astype(F32), ((0, LANES - lowrank), (0, 0)))
    kern = functools.partial(_gla_kernel, rb=rb)
    return pl.pallas_call(
        kern,
        grid=(l // rb,),
        in_specs=[pl.BlockSpec((rb, dk_all), lambda i: (i, 0)),
                  pl.BlockSpec((rb, dk_all), lambda i: (i, 1)),
                  pl.BlockSpec((rb, dv_all), lambda i: (i, 1)),
                  pl.BlockSpec((rb, dv_all), lambda i: (i, 2)),
                  pl.BlockSpec((rb, LANES), lambda i: (i, 0)),
                  pl.BlockSpec((LANES, dk_all), lambda i: (0, 0)),
                  pl.BlockSpec((1, dk_all), lambda i: (0, 0)),
                  pl.BlockSpec((1, dv_all // nh), lambda i: (0, 0))],
        out_specs=pl.BlockSpec((rb, dv_all), lambda i: (i, 0)),
        out_shape=jax.ShapeDtypeStruct((l, dv_all), BF16),
        scratch_shapes=[pltpu.VMEM((rb, dk_all), BF16),
                        pltpu.VMEM((rb, dk_all), BF16),
                        pltpu.VMEM((rb // CHUNK, 8, dk_all), F32),
                        pltpu.VMEM((nh, dv_all // nh, dk_all // nh), F32)],
        compiler_params=_params(("arbitrary",)),
        name="gla_mixer",
    )(proj, proj, proj, proj, g_low, w2, gate_b.astype(F32).reshape(1, dk_all),
      norm_w.astype(F32).reshape(1, dv_all // nh))


def _out0_kernel(x_ref, ya_ref, yb_ref, gw_ref, gb_ref, wa_ref, wb_ref, pw_ref, gt_ref, o_ref):
    ya = ya_ref[...]
    gate = jax.nn.sigmoid(_dot(ya, gw_ref[...]) + gb_ref[...])
    ya = (ya.astype(F32) * gate).astype(BF16)
    m = _dot(ya, wa_ref[...]) + _dot(yb_ref[...], wb_ref[...])
    o_ref[...] = _postnorm_residual(x_ref[...], m, pw_ref[...], gt_ref[...])


def _out1_kernel(x_ref, y_ref, w_ref, pw_ref, gt_ref, o_ref):
    m = _dot(y_ref[...], w_ref[...])
    o_ref[...] = _postnorm_residual(x_ref[...], m, pw_ref[...], gt_ref[...])


def _const_spec(shape):
    return pl.BlockSpec(shape, lambda i: (0,) * len(shape), pipeline_mode=pl.Buffered(1))


def _out_project0(x, ya, yb, glu_w, glu_b, w_out, post_w, mod, tm):
    l, d = x.shape
    wa = ya.shape[1]
    wb = yb.shape[1]
    return pl.pallas_call(
        _out0_kernel,
        grid=(l // tm,),
        in_specs=[pl.BlockSpec((tm, d), lambda i: (i, 0)),
                  pl.BlockSpec((tm, wa), lambda i: (i, 0)),
                  pl.BlockSpec((tm, wb), lambda i: (i, 0)),
                  _const_spec((wa, wa)),
                  _const_spec((1, wa)),
                  pl.BlockSpec((wa, d), lambda i: (0, 0), pipeline_mode=pl.Buffered(1)),
                  pl.BlockSpec((wb, d), lambda i: (1, 0), pipeline_mode=pl.Buffered(1)),
                  _const_spec((1, d)),
                  pl.BlockSpec((1, d), lambda i: (0, 2))],
        out_specs=pl.BlockSpec((tm, d), lambda i: (i, 0)),
        out_shape=jax.ShapeDtypeStruct((l, d), F32),
        compiler_params=_params(("arbitrary",)),
        name="out_project0",
    )(x, ya, yb, glu_w, glu_b.astype(F32).reshape(1, wa), w_out, w_out, post_w.reshape(1, d), mod)


def _out_project1(x, y, w_out, post_w, mod, tm):
    l, d = x.shape
    w = y.shape[1]
    return pl.pallas_call(
        _out1_kernel,
        grid=(l // tm,),
        in_specs=[pl.BlockSpec((tm, d), lambda i: (i, 0)),
                  pl.BlockSpec((tm, w), lambda i: (i, 0)),
                  _const_spec((w, d)),
                  _const_spec((1, d)),
                  pl.BlockSpec((1, d), lambda i: (0, 2))],
        out_specs=pl.BlockSpec((tm, d), lambda i: (i, 0)),
        out_shape=jax.ShapeDtypeStruct((l, d), F32),
        compiler_params=_params(("arbitrary",)),
        name="out_project1",
    )(x, y, w_out, post_w.reshape(1, d), mod)


def _ffn_kernel(x_ref, nw_ref, sc_ref, sh_ref, wg_ref, wu_ref, wd_ref, pw_ref, gt_ref, o_ref,
                h_ref, a_ref, *, n_hid, n_out):
    j = pl.program_id(1)
    th = a_ref.shape[2]
    tn = wd_ref.shape[1]

    @pl.when(j == 0)
    def _():
        h_ref[...] = _prenorm(x_ref[...], nw_ref[...], sc_ref[...], sh_ref[...]).astype(BF16)

    @pl.when(j < n_hid)
    def _():
        h = h_ref[...]
        a_ref[j] = (_silu(_dot(h, wg_ref[...])) * _dot(h, wu_ref[...])).astype(BF16)

    for jj in range(n_out):
        @pl.when(j == n_hid + jj)
        def _():
            acc = _dot(a_ref[0], wd_ref[0:th, :])
            for k in range(1, n_hid):
                acc = acc + _dot(a_ref[k], wd_ref[k * th:(k + 1) * th, :])
            o_ref[:, jj * tn:(jj + 1) * tn] = acc

    @pl.when(j == n_hid + n_out - 1)
    def _():
        o_ref[...] = _postnorm_residual(x_ref[...], o_ref[...], pw_ref[...], gt_ref[...])


def _ffn_block(x, mod, pre_w, post_w, w_gate, w_up, w_down, tm, th, tn):
    l, d = x.shape
    hid = w_gate.shape[1]
    n_hid, n_out = hid // th, d // tn
    kern = functools.partial(_ffn_kernel, n_hid=n_hid, n_out=n_out)
    return pl.pallas_call(
        kern,
        grid=(l // tm, n_hid + n_out),
        in_specs=[pl.BlockSpec((tm, d), lambda i, j: (i, 0)),
                  pl.BlockSpec((1, d), lambda i, j: (0, 0)),
                  pl.BlockSpec((1, d), lambda i, j: (0, 4)),
                  pl.BlockSpec((1, d), lambda i, j: (0, 3)),
                  pl.BlockSpec((d, th), lambda i, j: (0, jnp.minimum(j, n_hid - 1))),
                  pl.BlockSpec((d, th), lambda i, j: (0, jnp.minimum(j, n_hid - 1))),
                  pl.BlockSpec((hid, tn), lambda i, j: (0, jnp.maximum(j - n_hid, 0))),
                  pl.BlockSpec((1, d), lambda i, j: (0, 0)),
                  pl.BlockSpec((1, d), lambda i, j: (0, 5))],
        out_specs=pl.BlockSpec((tm, d), lambda i, j: (i, 0)),
        out_shape=jax.ShapeDtypeStruct((l, d), F32),
        scratch_shapes=[pltpu.VMEM((tm, d), BF16),
                        pltpu.VMEM((n_hid, tm, th), BF16)],
        compiler_params=_params(("arbitrary", "arbitrary")),
        name="ffn_block",
    )(x, pre_w.reshape(1, d), mod, mod, w_gate, w_up, w_down, post_w.reshape(1, d), mod)


class _Tiles(NamedTuple):
    tm: int
    tp: int
    tc: int
    tf: int
    th: int
    tn: int
    rb: int
    cb: int


def _tiles(l):
    return _Tiles(tm=min(512, l), tp=min(1024, l), tc=2048, tf=min(512, l), th=512, tn=512,
                  rb=min(512, l), cb=min(512, l // S5_T))


def kernel(x, c, ada_w0, ada_b0, mix_pre0, mix_post0, ffn_pre0, ffn_post0, w_in0, s5_lambda_re, s5_lambda_im, s5_log_step, s5_b_re, s5_b_im, s5_c_re, s5_c_im, s5_d, s5_glu_w, s5_glu_b, gdn_conv_w, gdn_a_log, gdn_dt_bias, gdn_norm_w, w_out0, ffn_gate0, ffn_up0, ffn_down0, ada_w1, ada_b1, mix_pre1, mix_post1, ffn_pre1, ffn_post1, w_in1, gla_gate_w2, gla_gate_b, gla_norm_w, w_out1, ffn_gate1, ffn_up1, ffn_down1):
    bsz, l, d = x.shape
    assert bsz == 1
    x = x.reshape(l, d)
    t = _tiles(l)
    tm, tp, tc, tf, th, tn, rb, cb = t.tm, t.tp, t.tc, t.tf, t.th, t.tn, t.rb, t.cb
    n_hs = int(math.log2(cb))
    assert l % tm == 0 and l % tp == 0 and l % rb == 0 and l % (cb * S5_T) == 0 and (1 << n_hs) == cb

    s5_w = s5_glu_w.shape[0]
    qk_w = (GDN_V_HEADS // 2) * GDN_HEAD_DIM
    v_w = GDN_V_HEADS * GDN_HEAD_DIM
    n0 = s5_w + 2 * qk_w + 2 * v_w
    dk_all = gla_gate_w2.shape[1]
    dv_all = w_out1.shape[0]
    n1 = 2 * dk_all + 2 * dv_all

    mod0 = _ada_modulation(c, ada_w0, ada_b0)
    mod1 = _ada_modulation(c, ada_w1, ada_b1)
    proj0, ab = _prenorm_project(x, mod0, 0, mix_pre0, w_in0, n0, tp, tc)
    y_a = _s5_mixer(proj0, s5_lambda_re, s5_lambda_im, s5_log_step, s5_b_re, s5_b_im, s5_c_re, s5_c_im, s5_d,
                    l, cb, n_hs)
    y_b = _gdn_mixer(proj0, ab, gdn_conv_w, gdn_a_log, gdn_dt_bias, gdn_norm_w, l, rb)
    x = _out_project0(x, y_a, y_b, s5_glu_w.astype(BF16), s5_glu_b, w_out0.astype(BF16), mix_post0, mod0, tm)
    x = _ffn_block(x, mod0, ffn_pre0, ffn_post0, ffn_gate0.astype(BF16), ffn_up0.astype(BF16),
                   ffn_down0.astype(BF16), tf, th, tn)

    proj1, g_low = _prenorm_project(x, mod1, 0, mix_pre1, w_in1, n1, tp, tc)
    y_c = _gla_mixer(proj1, g_low, gla_gate_w2, gla_gate_b, gla_norm_w, l, rb, dk_all, dv_all)
    x = _out_project1(x, y_c, w_out1.astype(BF16), mix_post1, mod1, tm)
    x = _ffn_block(x, mod1, ffn_pre1, ffn_post1, ffn_gate1.astype(BF16), ffn_up1.astype(BF16),
                   ffn_down1.astype(BF16), tf, th, tn)
    return x.reshape(bsz, l, d)
```

```python
import functools
import math
from typing import NamedTuple

import jax
import jax.numpy as jnp
from jax import lax
from jax.experimental import pallas as pl
from jax.experimental.pallas import tpu as pltpu

F32 = jnp.float32
BF16 = jnp.bfloat16
EPS = 1e-6

LANES = 128
MXU_N = 256
VMEM_LIMIT = 56 << 20

S5_GROUP = 16
S5_STATE = 64
S5_T = 16
GDN_HEAD_DIM = 128
GDN_V_HEADS = 8
GDN_CONV = 4
CHUNK = 64
INV_BASE = 16
PREP = 8
GLA_HEADS = 4
GLA_TAU = 16.0
GLA_GROUP = 2


def _params(sem):
    return pltpu.CompilerParams(dimension_semantics=sem, vmem_limit_bytes=VMEM_LIMIT)


def _dot(a, b):
    return jnp.dot(a, b, preferred_element_type=F32)


def _dot_nt(a, b):
    return lax.dot_general(a, b, (((1,), (1,)), ((), ())), preferred_element_type=F32)


def _dot_tn(a, b):
    return lax.dot_general(a, b, (((0,), (0,)), ((), ())), preferred_element_type=F32)


def _split_bf16(x):
    hi = x.astype(BF16)
    lo = (x - hi.astype(F32)).astype(BF16)
    return hi, lo


def _dot_split(a, b):
    ah, al = _split_bf16(a)
    bh, bl = _split_bf16(b)
    return _dot(ah, bh) + (_dot(ah, bl) + _dot(al, bh))


def _silu(x):
    return x * jax.nn.sigmoid(x)


def _prenorm(x, nw, sc, sh):
    ms = jnp.mean(x * x, axis=-1, keepdims=True)
    return (x * lax.rsqrt(ms + EPS)) * (nw * (1.0 + sc)) + sh


def _postnorm_residual(x, m, pw, gt):
    ms = jnp.mean(m * m, axis=-1, keepdims=True)
    return x + (m * lax.rsqrt(ms + EPS)) * (gt * pw)


def _shift_rows(x, s, row):
    return jnp.where(row >= s, pltpu.roll(x, s, 0), 0.0)


def _ada_kernel(c_ref, w_ref, b_ref, o_ref):
    s = _silu(c_ref[...])
    o_ref[...] = jnp.sum(s * w_ref[...], axis=0, keepdims=True) + b_ref[...]


def _ada_modulation(c, w, b):
    d, n = w.shape
    tn = 1024
    return pl.pallas_call(
        _ada_kernel,
        grid=(n // tn,),
        in_specs=[pl.BlockSpec((d, 1), lambda j: (0, 0)),
                  pl.BlockSpec((d, tn), lambda j: (0, j)),
                  pl.BlockSpec((1, tn), lambda j: (0, j))],
        out_specs=pl.BlockSpec((1, tn), lambda j: (0, j)),
        out_shape=jax.ShapeDtypeStruct((1, n), F32),
        compiler_params=_params(("arbitrary",)),
        name="ada_modulation",
    )(c.reshape(d, 1), w, b.reshape(1, n))


def _proj_kernel(x_ref, nw_ref, sc_ref, sh_ref, w_ref, ws_ref, o_ref, os_ref, h_ref):
    @pl.when(pl.program_id(1) == 0)
    def _():
        hb = _prenorm(x_ref[...], nw_ref[...], sc_ref[...], sh_ref[...]).astype(BF16)
        h_ref[...] = hb
        os_ref[...] = _dot(hb, ws_ref[...])

    o_ref[...] = _dot(h_ref[...], w_ref[...]).astype(o_ref.dtype)


def _prenorm_project(x, mod, mod_base, norm_w, w_all, n, tm, tn):
    l, d = x.shape
    w_main = w_all.astype(BF16)
    w_small = jnp.pad(w_all[:, n:], ((0, 0), (0, LANES - (w_all.shape[1] - n)))).astype(BF16)
    ns = LANES
    return pl.pallas_call(
        _proj_kernel,
        grid=(l // tm, n // tn),
        in_specs=[pl.BlockSpec((tm, d), lambda i, j: (i, 0)),
                  pl.BlockSpec((1, d), lambda i, j: (0, 0)),
                  pl.BlockSpec((1, d), lambda i, j: (0, mod_base + 1)),
                  pl.BlockSpec((1, d), lambda i, j: (0, mod_base)),
                  pl.BlockSpec((d, tn), lambda i, j: (0, j)),
                  pl.BlockSpec((d, ns), lambda i, j: (0, 0))],
        out_specs=[pl.BlockSpec((tm, tn), lambda i, j: (i, j)),
                   pl.BlockSpec((tm, ns), lambda i, j: (i, 0))],
        out_shape=[jax.ShapeDtypeStruct((l, n), BF16), jax.ShapeDtypeStruct((l, ns), F32)],
        scratch_shapes=[pltpu.VMEM((tm, d), BF16)],
        compiler_params=_params(("arbitrary", "arbitrary")),
        name="prenorm_project",
    )(x, norm_w.reshape(1, d), mod, mod, w_main, w_small)


def _s5_build_tables(b_ref, c_ref, lam_ref, d_ref, wi_ref, wb_ref, wc_ref, hs_ref, n_hs):
    t = S5_T
    half = b_ref.shape[2] // 2
    lr, li = lam_ref[0, 0:1, :], lam_ref[0, 1:2, :]
    dt = jnp.exp(lam_ref[0, 2:3, :])
    mag = jnp.exp(lr * dt)
    ar, ai = mag * jnp.cos(li * dt), mag * jnp.sin(li * dt)
    den = lr * lr + li * li
    nr, ni = ar - 1.0, ai
    f_re = (nr * lr + ni * li) / den
    f_im = (ni * lr - nr * li) / den
    b_re, b_im = b_ref[0, :, :half], b_ref[0, :, half:]
    bb_re = f_re * b_re - f_im * b_im
    bb_im = f_re * b_im + f_im * b_re
    c_re, c_im = c_ref[0, :, :half], c_ref[0, :, half:]
    bb_hi, bb_lo = _split_bf16(jnp.concatenate([bb_re, bb_im], axis=1))
    rr = lax.broadcasted_iota(jnp.int32, (LANES, LANES), 0)
    qq = lax.broadcasted_iota(jnp.int32, (LANES, LANES), 1)
    skip = jnp.where(rr == qq, d_ref[0], 0.0)

    pr, pi = jnp.ones_like(ar), jnp.zeros_like(ar)
    pows = []
    for d in range(t + 1):
        pows.append((pr, pi))
        cs = jnp.concatenate([c_re * pr - c_im * pi, -(c_re * pi + c_im * pr)], axis=1)
        if d < t:
            cs_hi, cs_lo = _split_bf16(cs)
            k_d = _dot_nt(bb_hi, cs_hi) + (_dot_nt(bb_hi, cs_lo) + _dot_nt(bb_lo, cs_hi))
            if d == 0:
                k_d = k_d + skip
            k_d = k_d.astype(BF16)
            for s in range(t - d):
                wi_ref[s * LANES:(s + 1) * LANES, (s + d) * LANES:(s + d + 1) * LANES] = k_d
        if d >= 1:
            wc_ref[:, (d - 1) * LANES:d * LANES] = cs.T.astype(BF16)
        pr, pi = pr * ar - pi * ai, pr * ai + pi * ar
    for tb in range(t * LANES // MXU_N):
        wi_ref[(2 * tb + 1) * LANES:(2 * tb + 2) * LANES, 2 * tb * LANES:(2 * tb + 1) * LANES] = (
            jnp.zeros((LANES, LANES), BF16))
    for s in range(t):
        pr, pi = pows[t - 1 - s]
        wb_ref[s * LANES:(s + 1) * LANES, :half] = (bb_re * pr - bb_im * pi).astype(BF16)
        wb_ref[s * LANES:(s + 1) * LANES, half:] = (bb_re * pi + bb_im * pr).astype(BF16)
    pr, pi = pows[t]
    for k in range(n_hs):
        hs_ref[k:k + 1, :half] = pr
        hs_ref[k:k + 1, half:] = pi
        pr, pi = pr * pr - pi * pi, 2.0 * pr * pi


def _s5_kernel(u_ref, b_ref, c_ref, lam_ref, d_ref, o_ref,
               wi_ref, wb_ref, wc_ref, hs_ref, uf_ref, u2_ref, yf_ref, carry_ref, *, cb, n_hs):
    t = S5_T
    half = carry_ref.shape[1] // 2

    @pl.when(pl.program_id(1) == 0)
    def _():
        carry_ref[...] = jnp.zeros_like(carry_ref)
        _s5_build_tables(b_ref, c_ref, lam_ref, d_ref, wi_ref, wb_ref, wc_ref, hs_ref, n_hs)

    uf_ref[...] = u_ref[...].astype(F32)
    for k in range(t):
        u2_ref[:, k * LANES:(k + 1) * LANES] = uf_ref[pl.ds(k, cb, stride=t), :].astype(BF16)
    u2 = u2_ref[...]

    s = _dot(u2, wb_ref[...])
    n_tb = t * LANES // MXU_N
    intra = [_dot(u2[:, :(tb + 1) * MXU_N], wi_ref[:(tb + 1) * MXU_N, tb * MXU_N:(tb + 1) * MXU_N])
             for tb in range(n_tb)]
    sr, si = s[:, :half], s[:, half:]
    row = lax.broadcasted_iota(jnp.int32, (cb, half), 0)
    cr, ci = carry_ref[:, :half], carry_ref[:, half:]
    pr, pi = hs_ref[0:1, :half], hs_ref[0:1, half:]
    sr = sr + jnp.where(row == 0, pr * cr - pi * ci, 0.0)
    si = si + jnp.where(row == 0, pr * ci + pi * cr, 0.0)
    for k in range(n_hs):
        pr, pi = hs_ref[k:k + 1, :half], hs_ref[k:k + 1, half:]
        shr, shi = _shift_rows(sr, 1 << k, row), _shift_rows(si, 1 << k, row)
        sr, si = sr + pr * shr - pi * shi, si + pr * shi + pi * shr
    epr = jnp.where(row >= 1, pltpu.roll(sr, 1, 0), cr)
    epi = jnp.where(row >= 1, pltpu.roll(si, 1, 0), ci)
    carry_ref[:, :half] = sr[cb - 1:cb]
    carry_ref[:, half:] = si[cb - 1:cb]
    ep = jnp.concatenate([epr, epi], axis=1).astype(BF16)
    y_inter = _dot(ep, wc_ref[...])

    for tb in range(n_tb):
        yb = jax.nn.gelu(y_inter[:, tb * MXU_N:(tb + 1) * MXU_N] + intra[tb], approximate=True)
        for q in range(MXU_N // LANES):
            yf_ref[pl.ds(tb * (MXU_N // LANES) + q, cb, stride=t), :] = yb[:, q * LANES:(q + 1) * LANES]
    o_ref[...] = yf_ref[...].astype(o_ref.dtype)


def _s5_group_rows(x, nl):
    g, p, n = x.shape
    gl = g // nl
    x = jnp.tile(x.astype(F32).reshape(nl, gl * p, n), (1, 1, gl))
    rg = jnp.arange(gl * p)[:, None] // p
    cg = jnp.arange(gl * n)[None, :] // n
    return jnp.where(rg == cg, x, 0.0)


def _s5_mixer(proj, lam_re, lam_im, log_step, b_re, b_im, c_re, c_im, d_skip, l, cb, n_hs):
    t = S5_T
    g, n = lam_re.shape
    nl = g * S5_GROUP // LANES
    half = (g // nl) * n
    rb = cb * t
    brow = jnp.concatenate([_s5_group_rows(b_re.transpose(0, 2, 1), nl),
                            _s5_group_rows(b_im.transpose(0, 2, 1), nl)], axis=-1)
    crow = jnp.concatenate([_s5_group_rows(c_re, nl), _s5_group_rows(c_im, nl)], axis=-1)
    lam = jnp.stack([lam_re.astype(F32).reshape(nl, half), lam_im.astype(F32).reshape(nl, half),
                     jnp.repeat(log_step.astype(F32), n).reshape(nl, half)], axis=1)
    dsk = d_skip.astype(F32).reshape(nl, 1, LANES)
    kern = functools.partial(_s5_kernel, cb=cb, n_hs=n_hs)
    return pl.pallas_call(
        kern,
        grid=(nl, l // rb),
        in_specs=[pl.BlockSpec((rb, LANES), lambda a, b: (b, a)),
                  pl.BlockSpec((1, LANES, 2 * half), lambda a, b: (a, 0, 0)),
                  pl.BlockSpec((1, LANES, 2 * half), lambda a, b: (a, 0, 0)),
                  pl.BlockSpec((1, 3, half), lambda a, b: (a, 0, 0)),
                  pl.BlockSpec((1, 1, LANES), lambda a, b: (a, 0, 0))],
        out_specs=pl.BlockSpec((rb, LANES), lambda a, b: (b, a)),
        out_shape=jax.ShapeDtypeStruct((l, nl * LANES), BF16),
        scratch_shapes=[pltpu.VMEM((t * LANES, t * LANES), BF16),
                        pltpu.VMEM((t * LANES, 2 * half), BF16),
                        pltpu.VMEM((2 * half, t * LANES), BF16),
                        pltpu.VMEM((n_hs, 2 * half), F32),
                        pltpu.VMEM((rb, LANES), F32),
                        pltpu.VMEM((cb, t * LANES), BF16),
                        pltpu.VMEM((rb, LANES), F32),
                        pltpu.VMEM((1, 2 * half), F32)],
        compiler_params=_params(("arbitrary", "arbitrary")),
        name="s5_mixer",
    )(proj, brow, crow, lam, dsk)


def _unit_lower_inverse(lows):
    c = lows[0][0].shape[0]
    nb = len(lows[0])
    r = lax.broadcasted_iota(jnp.int32, (c, nb * c), 0)
    q = lax.broadcasted_iota(jnp.int32, (c, nb * c), 1)
    eye_w = jnp.where(q % c == r, 1.0, 0.0)
    rr = lax.broadcasted_iota(jnp.int32, (nb * c, nb * c), 0)
    qq = lax.broadcasted_iota(jnp.int32, (nb * c, nb * c), 1)
    on_diag = (rr // c) == (qq // c)

    def block_diag(xw):
        return jnp.where(on_diag, jnp.concatenate([xw] * nb, axis=0), 0.0).astype(BF16)

    def mm(xw, yw):
        return _dot(xw.astype(BF16), block_diag(yw))

    qc = q % c
    base = INV_BASE
    lws = [jnp.concatenate(group, axis=1) for group in lows]
    diag = [jnp.where(r // base == qc // base, lw, 0.0) for lw in lws]
    accs = [eye_w - d for d in diag]
    xs = [mm(d, d) for d in diag]
    steps = int(math.log2(base)) - 1
    for k in range(steps):
        last = k == steps - 1
        lhs = [acc if last else jnp.concatenate([x, acc], axis=0) for x, acc in zip(xs, accs)]
        both = [mm(a, x) for a, x in zip(lhs, xs)]
        accs = [acc + (b if last else b[c:]) for acc, b in zip(accs, both)]
        xs = [b[:c] for b in both]
    size = base
    while size < c:
        below = (r // (2 * size) == qc // (2 * size)) & (r // size != qc // size)
        subs = [jnp.where(below, lw, 0.0) for lw in lws]
        ys = [mm(s, acc) for s, acc in zip(subs, accs)]
        accs = [acc - mm(acc, y) for acc, y in zip(accs, ys)]
        size *= 2
    return [[acc[:, i * c:(i + 1) * c] for i in range(nb)] for acc in accs]


def _gdn_kernel(q_ref, k_ref, v_ref, z_ref, ab_ref, cw_ref, al_ref, dtb_ref, nw_ref, o_ref,
                xe_ref, qn_ref, kn_ref, vc_ref, gc_ref, beta_ref, s_ref,
                uv_ref, wk_ref, qd_ref, kd_ref, at_ref, gl_ref, *, rb):
    hd = GDN_HEAD_DIM
    nh = GDN_V_HEADS
    nqk = nh // 2
    qw = nqk * hd
    c = CHUNK
    halo = 8

    @pl.when(pl.program_id(0) == 0)
    def _():
        s_ref[...] = jnp.zeros_like(s_ref)
        xe_ref[0:halo, :] = jnp.zeros((halo, xe_ref.shape[1]), F32)

    xe_ref[halo:, 0:qw] = q_ref[...].astype(F32)
    xe_ref[halo:, qw:2 * qw] = k_ref[...].astype(F32)
    xe_ref[halo:, 2 * qw:] = v_ref[...].astype(F32)
    acc = cw_ref[GDN_CONV - 1:GDN_CONV, :] * xe_ref[halo:, :]
    for s in range(1, GDN_CONV):
        acc = acc + cw_ref[GDN_CONV - 1 - s:GDN_CONV - s, :] * xe_ref[halo - s:halo - s + rb, :]
    xe_ref[0:halo, :] = xe_ref[rb:rb + halo, :]
    qkv = _silu(acc)
    for p in range(nqk):
        qp = qkv[:, p * hd:(p + 1) * hd]
        qn_ref[:, p * hd:(p + 1) * hd] = qp * lax.rsqrt(jnp.sum(qp * qp, -1, keepdims=True) + EPS) * (hd ** -0.5)
        kp = qkv[:, qw + p * hd:qw + (p + 1) * hd]
        kn_ref[:, p * hd:(p + 1) * hd] = kp * lax.rsqrt(jnp.sum(kp * kp, -1, keepdims=True) + EPS)
    vc_ref[...] = qkv[:, 2 * qw:]

    ab = ab_ref[...]
    g = -jnp.exp(al_ref[...]) * jax.nn.softplus(ab + dtb_ref[...])
    beta_ref[...] = jax.nn.sigmoid(ab)
    row = lax.broadcasted_iota(jnp.int32, (rb, LANES), 0) % c
    for k in range(int(math.log2(c))):
        g = g + jnp.where(row >= (1 << k), pltpu.roll(g, 1 << k, 0), 0.0)
    gc_ref[...] = g

    ri = lax.broadcasted_iota(jnp.int32, (c, c), 0)
    ci = lax.broadcasted_iota(jnp.int32, (c, c), 1)
    causal = ri >= ci
    strict = ri > ci
    nw = nw_ref[...]

    def prepare_body(it, carry):
        rows = [pl.multiple_of((it * PREP + j) * c, c) for j in range(PREP)]
        scores = []
        for j, r0 in enumerate(rows):
            gcc = gc_ref[pl.ds(r0, c), :]
            gl_ref[it * PREP + j] = jnp.broadcast_to(jnp.exp(gcc[c - 1:c, :]), (8, LANES))
            for p in range(nqk):
                qp = qn_ref[pl.ds(r0, c), p * hd:(p + 1) * hd]
                kp = kn_ref[pl.ds(r0, c), p * hd:(p + 1) * hd]
                qk_kk = _dot_nt(jnp.concatenate([qp, kp], axis=0).astype(BF16), kp.astype(BF16))
                scores.append((r0, p, gcc, qp, kp, qk_kk))
        lows, rhss, where = [], [], []
        for r0, p, gcc, qp, kp, qk_kk in scores:
            gct = gcc.T
            bet = beta_ref[pl.ds(r0, c), :]
            for h in (2 * p, 2 * p + 1):
                cols = slice(h * hd, (h + 1) * hd)
                gcol = jnp.broadcast_to(gcc[:, h:h + 1], (c, hd))
                bcol = jnp.broadcast_to(bet[:, nh + h:nh + h + 1], (c, hd))
                dec = jnp.exp(jnp.where(causal, gcol[:, :c] - gct[h:h + 1, :], -jnp.inf))
                lows.append(jnp.where(strict, bcol[:, :c] * qk_kk[c:] * dec, 0.0))
                at_ref[pl.ds(r0, c), h * hd:h * hd + c] = (qk_kk[:c] * dec).astype(BF16)
                eg = jnp.exp(gcol)
                qd_ref[pl.ds(r0, c), cols] = (qp * eg).astype(BF16)
                kd_ref[pl.ds(r0, c), cols] = (kp * jnp.exp(gcc[c - 1:c, h:h + 1] - gcol)).astype(BF16)
                rhss.append(jnp.concatenate([vc_ref[pl.ds(r0, c), cols] * bcol, kp * (bcol * eg)],
                                            axis=1).astype(BF16))
                where.append((r0, cols))
        t_groups = _unit_lower_inverse([lows[i:i + 4] for i in range(0, len(lows), 4)])
        t_mats = [t for group in t_groups for t in group]
        uws = [_dot(t.astype(BF16), rhs) for t, rhs in zip(t_mats, rhss)]
        for uw, (r0, cols) in zip(uws, where):
            uv_ref[pl.ds(r0, c), cols] = uw[:, :hd]
            wk_ref[pl.ds(r0, c), cols] = uw[:, hd:].astype(BF16)
        return carry

    lax.fori_loop(0, rb // (PREP * c), prepare_body, 0)

    def state_body(n, carry):
        r0 = pl.multiple_of(n * c, c)
        gl = gl_ref[n]
        heads = [slice(h * hd, (h + 1) * hd) for h in range(nh)]
        sts = [s_ref[h] for h in range(nh)]
        ws_qs = [_dot(jnp.concatenate([wk_ref[pl.ds(r0, c), cols], qd_ref[pl.ds(r0, c), cols]], axis=0),
                      st.astype(BF16)) for cols, st in zip(heads, sts)]
        v16s = [(uv_ref[pl.ds(r0, c), cols] - wq[:c]).astype(BF16) for cols, wq in zip(heads, ws_qs)]
        upd = [_dot_tn(kd_ref[pl.ds(r0, c), cols], v16) for cols, v16 in zip(heads, v16s)]
        intra = [_dot(at_ref[pl.ds(r0, c), h * hd:h * hd + c], v16) for h, v16 in enumerate(v16s)]
        for h, cols in enumerate(heads):
            s_ref[h] = sts[h] * gl[0:1, h:h + 1] + upd[h]
            o = ws_qs[h][c:] + intra[h]
            zh = z_ref[pl.ds(r0, c), cols].astype(F32)
            on = o * lax.rsqrt(jnp.mean(o * o, -1, keepdims=True) + EPS) * nw
            o_ref[pl.ds(r0, c), cols] = (on * _silu(zh)).astype(o_ref.dtype)
        return carry

    lax.fori_loop(0, rb // c, state_body, 0, unroll=True)


def _gdn_mixer(proj, ab, conv_w, a_log, dt_bias, norm_w, l, rb):
    hd, nh = GDN_HEAD_DIM, GDN_V_HEADS
    qw = (nh // 2) * hd
    vw = nh * hd
    pad = LANES - nh
    al = jnp.pad(a_log.astype(F32), (0, pad)).reshape(1, LANES)
    dtb = jnp.pad(dt_bias.astype(F32), (0, pad)).reshape(1, LANES)
    kern = functools.partial(_gdn_kernel, rb=rb)
    return pl.pallas_call(
        kern,
        grid=(l // rb,),
        in_specs=[pl.BlockSpec((rb, qw), lambda i: (i, 2)),
                  pl.BlockSpec((rb, qw), lambda i: (i, 3)),
                  pl.BlockSpec((rb, vw), lambda i: (i, 2)),
                  pl.BlockSpec((rb, vw), lambda i: (i, 3)),
                  pl.BlockSpec((rb, LANES), lambda i: (i, 0)),
                  pl.BlockSpec((GDN_CONV, 2 * qw + vw), lambda i: (0, 0)),
                  pl.BlockSpec((1, LANES), lambda i: (0, 0)),
                  pl.BlockSpec((1, LANES), lambda i: (0, 0)),
                  pl.BlockSpec((1, hd), lambda i: (0, 0))],
        out_specs=pl.BlockSpec((rb, vw), lambda i: (i, 0)),
        out_shape=jax.ShapeDtypeStruct((l, vw), BF16),
        scratch_shapes=[pltpu.VMEM((rb + 8, 2 * qw + vw), F32),
                        pltpu.VMEM((rb, qw), F32),
                        pltpu.VMEM((rb, qw), F32),
                        pltpu.VMEM((rb, vw), F32),
                        pltpu.VMEM((rb, LANES), F32),
                        pltpu.VMEM((rb, LANES), F32),
                        pltpu.VMEM((nh, hd, hd), F32),
                        pltpu.VMEM((rb, vw), F32),
                        pltpu.VMEM((rb, vw), BF16),
                        pltpu.VMEM((rb, vw), BF16),
                        pltpu.VMEM((rb, vw), BF16),
                        pltpu.VMEM((rb, vw), BF16),
                        pltpu.VMEM((rb // CHUNK, 8, LANES), F32)],
        compiler_params=_params(("arbitrary",)),
        name="gdn_mixer",
    )(proj, proj, proj, proj, ab, conv_w.astype(F32), al, dtb, norm_w.astype(F32).reshape(1, hd))


def _gla_kernel(q_ref, k_ref, v_ref, r_ref, gl_ref, w2_ref, gb_ref, nw_ref, o_ref,
                qt_ref, kt_ref, gle_ref, st_ref, *, rb):
    nh = GLA_HEADS
    dk = q_ref.shape[1] // nh
    dv = v_ref.shape[1] // nh
    c = CHUNK

    @pl.when(pl.program_id(0) == 0)
    def _():
        st_ref[...] = jnp.zeros_like(st_ref)

    x = _dot_split(gl_ref[...], w2_ref[...]) + gb_ref[...]
    b = jax.nn.log_sigmoid(x) / GLA_TAU
    row = lax.broadcasted_iota(jnp.int32, b.shape, 0) % c
    for k in range(int(math.log2(c))):
        b = b + jnp.where(row >= (1 << k), pltpu.roll(b, 1 << k, 0), 0.0)
    q = q_ref[...].astype(F32) * (dk ** -0.5)
    kf = k_ref[...].astype(F32)
    qt_ref[...] = (q * jnp.exp(b)).astype(BF16)
    kt_ref[...] = (kf * jnp.exp(-b)).astype(BF16)
    for n in range(rb // c):
        gle_ref[n] = jnp.broadcast_to(jnp.exp(b[(n + 1) * c - 1:(n + 1) * c, :]), (8, b.shape[1]))

    ri = lax.broadcasted_iota(jnp.int32, (c, c), 0)
    ci = lax.broadcasted_iota(jnp.int32, (c, c), 1)
    causal = ri >= ci
    nw = nw_ref[...]

    def chunk_body(it, carry):
        sts = [st_ref[h] for h in range(nh)]
        work = []
        for j in range(GLA_GROUP):
            n = it * GLA_GROUP + j
            r0 = pl.multiple_of(n * c, c)
            qts = [qt_ref[pl.ds(r0, c), h * dk:(h + 1) * dk] for h in range(nh)]
            kts = [kt_ref[pl.ds(r0, c), h * dk:(h + 1) * dk] for h in range(nh)]
            vhs = [v_ref[pl.ds(r0, c), h * dv:(h + 1) * dv] for h in range(nh)]
            scores = [_dot_nt(qts[h], kts[h]) for h in range(nh)]
            upd = [_dot_tn(vhs[h], kts[h]) for h in range(nh)]
            work.append((n, r0, qts, vhs, scores, upd))
        for n, r0, qts, vhs, scores, upd in work:
            gle = gle_ref[n]
            inter = [_dot_nt(qts[h], sts[h].astype(BF16)) for h in range(nh)]
            intra = [_dot(jnp.where(causal, scores[h], 0.0).astype(BF16), vhs[h]) for h in range(nh)]
            sts = [(sts[h] + upd[h]) * gle[0:1, h * dk:(h + 1) * dk] for h in range(nh)]
            for h in range(nh):
                o = intra[h] + inter[h]
                rh = r_ref[pl.ds(r0, c), h * dv:(h + 1) * dv].astype(F32)
                on = o * lax.rsqrt(jnp.mean(o * o, -1, keepdims=True) + EPS) * nw
                o_ref[pl.ds(r0, c), h * dv:(h + 1) * dv] = (on * _silu(rh)).astype(o_ref.dtype)
        for h in range(nh):
            st_ref[h] = sts[h]
        return carry

    lax.fori_loop(0, rb // (GLA_GROUP * c), chunk_body, 0, unroll=True)


def _gla_mixer(proj, g_low, gate_w2, gate_b, norm_w, l, rb, dk_all, dv_all):
    nh = GLA_HEADS
    lowrank = gate_w2.shape[0]
    w2 = jnp.pad(gate_w2.astype(F32), ((0, LANES - lowrank), (0, 0)))
    kern = functools.partial(_gla_kernel, rb=rb)
    return pl.pallas_call(
        kern,
        grid=(l // rb,),
        in_specs=[pl.BlockSpec((rb, dk_all), lambda i: (i, 0)),
                  pl.BlockSpec((rb, dk_all), lambda i: (i, 1)),
                  pl.BlockSpec((rb, dv_all), lambda i: (i, 1)),
                  pl.BlockSpec((rb, dv_all), lambda i: (i, 2)),
                  pl.BlockSpec((rb, LANES), lambda i: (i, 0)),
                  pl.BlockSpec((LANES, dk_all), lambda i: (0, 0)),
                  pl.BlockSpec((1, dk_all), lambda i: (0, 0)),
                  pl.BlockSpec((1, dv_all // nh), lambda i: (0, 0))],
        out_specs=pl.BlockSpec((rb, dv_all), lambda i: (i, 0)),
        out_shape=jax.ShapeDtypeStruct((l, dv_all), BF16),
        scratch_shapes=[pltpu.VMEM((rb, dk_all), BF16),
                        pltpu.VMEM((rb, dk_all), BF16),
                        pltpu.VMEM((rb // CHUNK, 8, dk_all), F32),
                        pltpu.VMEM((nh, dv_all // nh, dk_all // nh), F32)],
        compiler_params=_params(("arbitrary",)),
        name="gla_mixer",
    )(proj, proj, proj, proj, g_low, w2, gate_b.astype(F32).reshape(1, dk_all),
      norm_w.astype(F32).reshape(1, dv_all // nh))


def _out0_kernel(x_ref, ya_ref, yb_ref, gw_ref, gb_ref, wa_ref, wb_ref, pw_ref, gt_ref, o_ref):
    ya = ya_ref[...]
    gate = jax.nn.sigmoid(_dot(ya, gw_ref[...]) + gb_ref[...])
    ya = (ya.astype(F32) * gate).astype(BF16)
    m = _dot(ya, wa_ref[...]) + _dot(yb_ref[...], wb_ref[...])
    o_ref[...] = _postnorm_residual(x_ref[...], m, pw_ref[...], gt_ref[...])


def _out1_kernel(x_ref, y_ref, w_ref, pw_ref, gt_ref, o_ref):
    m = _dot(y_ref[...], w_ref[...])
    o_ref[...] = _postnorm_residual(x_ref[...], m, pw_ref[...], gt_ref[...])


def _const_spec(shape):
    return pl.BlockSpec(shape, lambda i: (0,) * len(shape), pipeline_mode=pl.Buffered(1))


def _out_project0(x, ya, yb, glu_w, glu_b, w_out, post_w, mod, tm):
    l, d = x.shape
    wa = ya.shape[1]
    wb = yb.shape[1]
    return pl.pallas_call(
        _out0_kernel,
        grid=(l // tm,),
        in_specs=[pl.BlockSpec((tm, d), lambda i: (i, 0)),
                  pl.BlockSpec((tm, wa), lambda i: (i, 0)),
                  pl.BlockSpec((tm, wb), lambda i: (i, 0)),
                  _const_spec((wa, wa)),
                  _const_spec((1, wa)),
                  pl.BlockSpec((wa, d), lambda i: (0, 0), pipeline_mode=pl.Buffered(1)),
                  pl.BlockSpec((wb, d), lambda i: (1, 0), pipeline_mode=pl.Buffered(1)),
                  _const_spec((1, d)),
                  pl.BlockSpec((1, d), lambda i: (0, 2))],
        out_specs=pl.BlockSpec((tm, d), lambda i: (i, 0)),
        out_shape=jax.ShapeDtypeStruct((l, d), F32),
        compiler_params=_params(("arbitrary",)),
        name="out_project0",
    )(x, ya, yb, glu_w, glu_b.astype(F32).reshape(1, wa), w_out, w_out, post_w.reshape(1, d), mod)


def _out_project1(x, y, w_out, post_w, mod, tm):
    l, d = x.shape
    w = y.shape[1]
    return pl.pallas_call(
        _out1_kernel,
        grid=(l // tm,),
        in_specs=[pl.BlockSpec((tm, d), lambda i: (i, 0)),
                  pl.BlockSpec((tm, w), lambda i: (i, 0)),
                  _const_spec((w, d)),
                  _const_spec((1, d)),
                  pl.BlockSpec((1, d), lambda i: (0, 2))],
        out_specs=pl.BlockSpec((tm, d), lambda i: (i, 0)),
        out_shape=jax.ShapeDtypeStruct((l, d), F32),
        compiler_params=_params(("arbitrary",)),
        name="out_project1",
    )(x, y, w_out, post_w.reshape(1, d), mod)


def _ffn_kernel(x_ref, nw_ref, sc_ref, sh_ref, wg_ref, wu_ref, wd_ref, pw_ref, gt_ref, o_ref,
                h_ref, a_ref, *, n_hid, n_out):
    j = pl.program_id(1)
    th = a_ref.shape[2]
    tn = wd_ref.shape[1]

    @pl.when(j == 0)
    def _():
        h_ref[...] = _prenorm(x_ref[...], nw_ref[...], sc_ref[...], sh_ref[...]).astype(BF16)

    @pl.when(j < n_hid)
    def _():
        half = h_ref.shape[0] // 2
        for r in (slice(0, half), slice(half, 2 * half)):
            h = h_ref[r, :]
            a_ref[j, r, :] = (_silu(_dot(h, wg_ref[...])) * _dot(h, wu_ref[...])).astype(BF16)

    for jj in range(n_out):
        @pl.when(j == n_hid + jj)
        def _():
            acc = _dot(a_ref[0], wd_ref[0:th, :])
            for k in range(1, n_hid):
                acc = acc + _dot(a_ref[k], wd_ref[k * th:(k + 1) * th, :])
            o_ref[:, jj * tn:(jj + 1) * tn] = acc

    @pl.when(j == n_hid + n_out - 1)
    def _():
        o_ref[...] = _postnorm_residual(x_ref[...], o_ref[...], pw_ref[...], gt_ref[...])


def _ffn_block(x, mod, pre_w, post_w, w_gate, w_up, w_down, tm, th, tn):
    l, d = x.shape
    hid = w_gate.shape[1]
    n_hid, n_out = hid // th, d // tn
    kern = functools.partial(_ffn_kernel, n_hid=n_hid, n_out=n_out)
    return pl.pallas_call(
        kern,
        grid=(l // tm, n_hid + n_out),
        in_specs=[pl.BlockSpec((tm, d), lambda i, j: (i, 0)),
                  pl.BlockSpec((1, d), lambda i, j: (0, 0)),
                  pl.BlockSpec((1, d), lambda i, j: (0, 4)),
                  pl.BlockSpec((1, d), lambda i, j: (0, 3)),
                  pl.BlockSpec((d, th), lambda i, j: (0, jnp.minimum(j, n_hid - 1))),
                  pl.BlockSpec((d, th), lambda i, j: (0, jnp.minimum(j, n_hid - 1))),
                  pl.BlockSpec((hid, tn), lambda i, j: (0, jnp.maximum(j - n_hid, 0))),
                  pl.BlockSpec((1, d), lambda i, j: (0, 0)),
                  pl.BlockSpec((1, d), lambda i, j: (0, 5))],
        out_specs=pl.BlockSpec((tm, d), lambda i, j: (i, 0)),
        out_shape=jax.ShapeDtypeStruct((l, d), F32),
        scratch_shapes=[pltpu.VMEM((tm, d), BF16),
                        pltpu.VMEM((n_hid, tm, th), BF16)],
        compiler_params=_params(("arbitrary", "arbitrary")),
        name="ffn_block",
    )(x, pre_w.reshape(1, d), mod, mod, w_gate, w_up, w_down, post_w.reshape(1, d), mod)


class _Tiles(NamedTuple):
    tm: int
    tp: int
    tc: int
    tf: int
    th: int
    tn: int
    rb: int
    cb: int


def _tiles(l):
    return _Tiles(tm=min(512, l), tp=min(1024, l), tc=2048, tf=min(512, l), th=512, tn=512,
                  rb=min(512, l), cb=min(512, l // S5_T))


def kernel(x, c, ada_w0, ada_b0, mix_pre0, mix_post0, ffn_pre0, ffn_post0, w_in0, s5_lambda_re, s5_lambda_im, s5_log_step, s5_b_re, s5_b_im, s5_c_re, s5_c_im, s5_d, s5_glu_w, s5_glu_b, gdn_conv_w, gdn_a_log, gdn_dt_bias, gdn_norm_w, w_out0, ffn_gate0, ffn_up0, ffn_down0, ada_w1, ada_b1, mix_pre1, mix_post1, ffn_pre1, ffn_post1, w_in1, gla_gate_w2, gla_gate_b, gla_norm_w, w_out1, ffn_gate1, ffn_up1, ffn_down1):
    bsz, l, d = x.shape
    assert bsz == 1
    x = x.reshape(l, d)
    t = _tiles(l)
    tm, tp, tc, tf, th, tn, rb, cb = t.tm, t.tp, t.tc, t.tf, t.th, t.tn, t.rb, t.cb
    n_hs = int(math.log2(cb))
    assert l % tm == 0 and l % tp == 0 and l % rb == 0 and l % (cb * S5_T) == 0 and (1 << n_hs) == cb

    s5_w = s5_glu_w.shape[0]
    qk_w = (GDN_V_HEADS // 2) * GDN_HEAD_DIM
    v_w = GDN_V_HEADS * GDN_HEAD_DIM
    n0 = s5_w + 2 * qk_w + 2 * v_w
    dk_all = gla_gate_w2.shape[1]
    dv_all = w_out1.shape[0]
    n1 = 2 * dk_all + 2 * dv_all

    mod0 = _ada_modulation(c, ada_w0, ada_b0)
    mod1 = _ada_modulation(c, ada_w1, ada_b1)
    proj0, ab = _prenorm_project(x, mod0, 0, mix_pre0, w_in0, n0, tp, tc)
    y_a = _s5_mixer(proj0, s5_lambda_re, s5_lambda_im, s5_log_step, s5_b_re, s5_b_im, s5_c_re, s5_c_im, s5_d,
                    l, cb, n_hs)
    y_b = _gdn_mixer(proj0, ab, gdn_conv_w, gdn_a_log, gdn_dt_bias, gdn_norm_w, l, rb)
    x = _out_project0(x, y_a, y_b, s5_glu_w.astype(BF16), s5_glu_b, w_out0.astype(BF16), mix_post0, mod0, tm)
    x = _ffn_block(x, mod0, ffn_pre0, ffn_post0, ffn_gate0.astype(BF16), ffn_up0.astype(BF16),
                   ffn_down0.astype(BF16), tf, th, tn)

    proj1, g_low = _prenorm_project(x, mod1, 0, mix_pre1, w_in1, n1, tp, tc)
    y_c = _gla_mixer(proj1, g_low, gla_gate_w2, gla_gate_b, gla_norm_w, l, rb, dk_all, dv_all)
    x = _out_project1(x, y_c, w_out1.astype(BF16), mix_post1, mod1, tm)
    x = _ffn_block(x, mod1, ffn_pre1, ffn_post1, ffn_gate1.astype(BF16), ffn_up1.astype(BF16),
                   ffn_down1.astype(BF16), tf, th, tn)
    return x.reshape(bsz, l, d)
```
